```python
import math
import jax, jax.numpy as jnp
from jax import lax
import numpy as np

D_MODEL = 1024
BATCH = 4
SEQ = 4096
DEPTH = 4
DEC_BATCH = 128
DEC_SEQ = 8
PAST_LEN = 2048
PAGE_SIZE = 128

F32 = jnp.float32
EPS = 1e-6
N_EVEN = (DEPTH + 1) // 2
N_ODD = DEPTH // 2
S5_CH = D_MODEL
S5_GROUP = 16
S5_GROUPS = S5_CH // S5_GROUP
S5_STATE = 64
S5_DT_MIN = 0.001
S5_DT_MAX = 0.1
S5_CHUNK = 128
ATTN_HEADS = 8
HEAD_DIM = D_MODEL // (2 * ATTN_HEADS)
ROT_DIM = HEAD_DIM // 4
ROPE_THETA = 500000.0
Q_BLOCK = 128
QKV_W = ATTN_HEADS * 2 * HEAD_DIM
D_IN_EVEN = S5_CH + 3 * QKV_W
D_OUT_EVEN = S5_CH + QKV_W
D_RNN = 1280
LRU_BLOCKS = 10
LRU_BS = D_RNN // LRU_BLOCKS
CONV_W = 4
LRU_C = 8.0
PEER_HEADS = 8
PEER_NKEYS = 128
PEER_EXPERTS = PEER_NKEYS * PEER_NKEYS
PEER_DKEY = 256
PEER_TOPK = 16
PEER_TBLOCK = 128

kernel_name = 'hybrid_s5_diffattn_rglru_peer_step'


def rmsnorm(x, g):
    xf = x.astype(F32)
    y = xf * lax.rsqrt(jnp.mean(xf * xf, axis=-1, keepdims=True) + EPS)
    return (y * g.astype(F32)).astype(x.dtype)


def rope_partial(x, pos):
    inv = jnp.power(ROPE_THETA, -jnp.arange(0, ROT_DIM, 2, dtype=F32) / ROT_DIM)
    ang = pos.astype(F32)[:, None] * inv[None, :]
    cos = jnp.cos(ang)[None, :, None, None, :]
    sin = jnp.sin(ang)[None, :, None, None, :]
    xr = x[..., :ROT_DIM].astype(F32)
    x1, x2 = xr[..., :ROT_DIM // 2], xr[..., ROT_DIM // 2:]
    rot = jnp.concatenate([x1 * cos - x2 * sin, x2 * cos + x1 * sin], axis=-1).astype(x.dtype)
    return jnp.concatenate([rot, x[..., ROT_DIM:]], axis=-1)


def _cplx_combine(e1, e2):
    a1r, a1i, b1r, b1i = e1
    a2r, a2i, b2r, b2i = e2
    return (a2r * a1r - a2i * a1i, a2r * a1i + a2i * a1r,
            a2r * b1r - a2i * b1i + b2r, a2r * b1i + a2i * b1r + b2i)


def _lin_combine(e1, e2):
    a1, b1 = e1
    a2, b2 = e2
    return a1 * a2, a2 * b1 + b2


def s5_mixer(u, h0_re, h0_im, lam_re, lam_im, b_re, b_im, c_re, c_im, d, log_dt, w_glu, b_glu):
    B_, L, _ = u.shape
    lr, li = lam_re.astype(F32), lam_im.astype(F32)
    dt = jnp.exp(log_dt.astype(F32))[:, None]
    mag = jnp.exp(lr * dt)
    ab_re, ab_im = mag * jnp.cos(li * dt), mag * jnp.sin(li * dt)
    den = lr * lr + li * li
    fr = ((ab_re - 1.0) * lr + ab_im * li) / den
    fi = (ab_im * lr - (ab_re - 1.0) * li) / den
    br_w, bi_w = b_re.astype(F32), b_im.astype(F32)
    cr_w, ci_w = c_re.astype(F32), c_im.astype(F32)
    dd = d.astype(F32)
    ch = S5_CHUNK if L % S5_CHUNK == 0 else L
    nc = L // ch
    uc = u.astype(F32).reshape(B_, nc, ch, S5_GROUPS, S5_GROUP).transpose(1, 0, 2, 3, 4)

    def chunk(carry, u_c):
        hr0, hi0 = carry
        bu_re = jnp.einsum('blgp,gnp->blgn', u_c, br_w)
        bu_im = jnp.einsum('blgp,gnp->blgn', u_c, bi_w)
        xr_ = fr * bu_re - fi * bu_im
        xi_ = fr * bu_im + fi * bu_re
        xr_ = xr_.at[:, 0].add(ab_re * hr0 - ab_im * hi0)
        xi_ = xi_.at[:, 0].add(ab_re * hi0 + ab_im * hr0)
        ar = jnp.broadcast_to(ab_re, xr_.shape)
        ai = jnp.broadcast_to(ab_im, xr_.shape)
        _, _, hr, hi = lax.associative_scan(_cplx_combine, (ar, ai, xr_, xi_), axis=1)
        y = (jnp.einsum('blgn,gpn->blgp', hr, cr_w) - jnp.einsum('blgn,gpn->blgp', hi, ci_w)
             + dd * u_c)
        return (hr[:, -1], hi[:, -1]), y

    (hr_last, hi_last), y = lax.scan(chunk, (h0_re.astype(F32), h0_im.astype(F32)), uc)
    y = y.transpose(1, 0, 2, 3, 4).reshape(B_, L, S5_CH)
    g = jax.nn.gelu(y)
    out = g * jax.nn.sigmoid(g @ w_glu.astype(F32) + b_glu.astype(F32))
    return out.astype(u.dtype), hr_last, hi_last


def diff_attention(q, k, v, q_pos, k_pos, lam):
    scale = HEAD_DIM ** -0.5

    def block(args):
        qb, qp = args
        s = jnp.einsum('bqhsd,bkhsd->bhsqk', qb, k).astype(F32) * scale
        mask = k_pos[None, :] <= qp[:, None]
        p = jax.nn.softmax(jnp.where(mask, s, -jnp.inf), axis=-1)
        a = p[:, :, 0] - lam * p[:, :, 1]
        return jnp.einsum('bhqk,bkhe->bqhe', a.astype(v.dtype), v)

    B_, Lq = q.shape[0], q.shape[1]
    if Lq > Q_BLOCK and Lq % Q_BLOCK == 0:
        nb = Lq // Q_BLOCK
        qs = jnp.moveaxis(q.reshape(B_, nb, Q_BLOCK, ATTN_HEADS, 2, HEAD_DIM), 1, 0)
        ps = q_pos.reshape(nb, Q_BLOCK)
        o = lax.map(block, (qs, ps))
        return jnp.moveaxis(o, 0, 1).reshape(B_, Lq, ATTN_HEADS, 2 * HEAD_DIM)
    return block((q, q_pos))


def gather_past(cache_layer, page_table):
    db, n_pages = page_table.shape
    g = cache_layer[page_table].reshape(db, n_pages * PAGE_SIZE, 2, ATTN_HEADS, 2 * HEAD_DIM)
    past_k = g[:, :, 0].reshape(db, n_pages * PAGE_SIZE, ATTN_HEADS, 2, HEAD_DIM)
    return past_k, g[:, :, 1]


def even_mixer(h, pos, past, h0_re, h0_im, layer, j, p):
    B_, L, _ = h.shape
    proj = h @ p['w_in_even'][j]
    u = proj[..., :S5_CH]
    q = proj[..., S5_CH:S5_CH + QKV_W].reshape(B_, L, ATTN_HEADS, 2, HEAD_DIM)
    k = proj[..., S5_CH + QKV_W:S5_CH + 2 * QKV_W].reshape(B_, L, ATTN_HEADS, 2, HEAD_DIM)
    v = proj[..., S5_CH + 2 * QKV_W:].reshape(B_, L, ATTN_HEADS, 2 * HEAD_DIM)
    q = rope_partial(rmsnorm(q, p['q_norm'][j]), pos)
    k = rope_partial(rmsnorm(k, p['k_norm'][j]), pos)
    if past is None:
        k_all, v_all, k_pos = k, v, pos
    else:
        past_k, past_v = past
        k_all = jnp.concatenate([past_k.astype(k.dtype), k], axis=1)
        v_all = jnp.concatenate([past_v.astype(v.dtype), v], axis=1)
        k_pos = jnp.concatenate([jnp.arange(past_k.shape[1], dtype=jnp.int32), pos])
    lam_init = 0.8 - 0.6 * math.exp(-0.3 * layer)
    lam = (jnp.exp(jnp.sum(p['lambda_q1'][j].astype(F32) * p['lambda_k1'][j].astype(F32)))
           - jnp.exp(jnp.sum(p['lambda_q2'][j].astype(F32) * p['lambda_k2'][j].astype(F32)))
           + lam_init)
    o = diff_attention(q, k_all, v_all, pos, k_pos, lam)
    o = rmsnorm(o, p['attn_subln'][j]) * (1.0 - lam_init)
    y_s5, hr, hi = s5_mixer(u, h0_re, h0_im, p['s5_lambda_re'][j], p['s5_lambda_im'][j],
                            p['s5_b_re'][j], p['s5_b_im'][j], p['s5_c_re'][j], p['s5_c_im'][j],
                            p['s5_d'][j], p['s5_log_dt'][j], p['s5_w_glu'][j], p['s5_b_glu'][j])
    out = jnp.concatenate([y_s5, o.reshape(B_, L, QKV_W)], axis=-1) @ p['w_out_even'][j]
    rows = jnp.stack([k.reshape(B_, L, ATTN_HEADS, 2 * HEAD_DIM), v], axis=2)
    return out, rows, hr, hi


def rg_lru(x, h0, w_a, b_a, w_x, b_x, lam):
    B_, L, _ = x.shape
    xf = x.astype(F32)
    xb = xf.reshape(B_, L, LRU_BLOCKS, LRU_BS)
    r = jax.nn.sigmoid(jnp.einsum('blnc,ncd->blnd', xb, w_a.astype(F32)).reshape(B_, L, D_RNN)
                       + b_a.astype(F32))
    i = jax.nn.sigmoid(jnp.einsum('blnc,ncd->blnd', xb, w_x.astype(F32)).reshape(B_, L, D_RNN)
                       + b_x.astype(F32))
    log_a = -LRU_C * r * jax.nn.softplus(-lam.astype(F32))
    a = jnp.exp(log_a)
    b = jnp.sqrt(-jnp.expm1(2.0 * log_a)) * (i * xf)
    b = b.at[:, 0].add(a[:, 0] * h0.astype(F32))
    _, hs = lax.associative_scan(_lin_combine, (a, b), axis=1)
    return hs.astype(x.dtype), hs[:, -1]


def odd_mixer(h, conv_buf, h0, j, p):
    B_, L, _ = h.shape
    proj = h @ p['w_in_odd'][j]
    gate = jax.nn.gelu(proj[..., :D_RNN])
    xr = proj[..., D_RNN:]
    xc = jnp.concatenate([conv_buf.astype(xr.dtype), xr], axis=1)
    w = p['conv_w'][j]
    conv = p['conv_b'][j] + xc[:, 0:L] * w[0]
    for tap in range(1, CONV_W):
        conv = conv + xc[:, tap:tap + L] * w[tap]
    y, h_last = rg_lru(conv, h0, p['lru_w_a'][j], p['lru_b_a'][j], p['lru_w_x'][j],
                       p['lru_b_x'][j], p['lru_lambda'][j])
    out = (gate * y) @ p['w_out_odd'][j]
    return out, xc[:, L:], h_last


def peer_ffn(h, wq, keys, u_tab, v_tab):
    B_, L, D = h.shape
    T = B_ * L
    nb = -(-T // PEER_TBLOCK)
    xt = jnp.pad(h.reshape(T, D), ((0, nb * PEER_TBLOCK - T), (0, 0))).reshape(nb, PEER_TBLOCK, D)
    kf = keys.astype(F32)

    def block(xb):
        q = (xb @ wq).astype(F32).reshape(PEER_TBLOCK, PEER_HEADS, PEER_DKEY)
        q = q * lax.rsqrt(jnp.mean(q * q, axis=-1, keepdims=True) + EPS)
        q = q.reshape(PEER_TBLOCK, PEER_HEADS, 2, PEER_DKEY // 2)
        s = jnp.einsum('thsd,hskd->thsk', q, kf)
        sv, si = lax.top_k(s, PEER_TOPK)
        cand = (sv[:, :, 0, :, None] + sv[:, :, 1, None, :]).reshape(
            PEER_TBLOCK, PEER_HEADS, PEER_TOPK * PEER_TOPK)
        top, ci = lax.top_k(cand, PEER_TOPK)
        e1 = jnp.take_along_axis(si[:, :, 0], ci // PEER_TOPK, axis=-1)
        e2 = jnp.take_along_axis(si[:, :, 1], ci % PEER_TOPK, axis=-1)
        expert = e1 * PEER_NKEYS + e2
        g = jax.nn.softmax(top, axis=-1)
        act = jax.nn.gelu(jnp.einsum('td,thkd->thk', xb, u_tab[expert]).astype(F32))
        return jnp.einsum('thk,thkd->td', (g * act).astype(xb.dtype), v_tab[expert])

    out = lax.map(block, xt)
    return out.reshape(nb * PEER_TBLOCK, D)[:T].reshape(B_, L, D)


def _trunk(x, c, pos, cache_kv, page_table, s5_re0, s5_im0, lru0, conv0, p):
    kv_rows, s5_re, s5_im, lru_h, conv_bufs = [], [], [], [], []
    c_act = jax.nn.silu(c)
    for layer in range(DEPTH):
        j = layer // 2
        mod = c_act @ p['w_ada'][layer] + p['b_ada'][layer]
        sh1, sc1, g1, sh2, sc2, g2 = [m[:, None, :] for m in jnp.split(mod, 6, axis=-1)]
        hm = rmsnorm(x, p['norm_mix'][layer]) * (1.0 + sc1) + sh1
        if layer % 2 == 0:
            past = None if cache_kv is None else gather_past(cache_kv[j], page_table)
            out, rows, hr, hi = even_mixer(hm, pos, past, s5_re0[j], s5_im0[j], layer, j, p)
            kv_rows.append(rows)
            s5_re.append(hr)
            s5_im.append(hi)
        else:
            out, buf, hl = odd_mixer(hm, conv0[j], lru0[j], j, p)
            conv_bufs.append(buf)
            lru_h.append(hl)
        x = x + g1 * out
        hf = rmsnorm(x, p['norm_ffn'][layer]) * (1.0 + sc2) + sh2
        x = x + g2 * peer_ffn(hf, p['peer_wq'][layer], p['peer_keys'][layer],
                              p['peer_u'][layer], p['peer_v'][layer])
    return (x, jnp.stack(kv_rows), jnp.stack(s5_re), jnp.stack(s5_im),
            jnp.stack(lru_h), jnp.stack(conv_bufs))


def setup_inputs(seed: int = 0) -> dict:
    key = jax.random.key(seed)
    ks = list(jax.random.split(key, 48))

    def nrm(i, shape, scale):
        return jax.random.normal(ks[i], shape, F32) * scale

    n_pages = PAST_LEN // PAGE_SIZE
    n_used = DEC_BATCH * n_pages
    n_pool = n_used + max(1, n_used // 4)
    page_table = jax.random.permutation(ks[0], n_pool)[:n_used].reshape(DEC_BATCH, n_pages).astype(jnp.int32)
    s5_n = jnp.arange(S5_STATE, dtype=F32)
    lru_u = jax.random.uniform(ks[40], (N_ODD, D_RNN), F32, 0.9, 0.999)
    return {
        'x_prompt': nrm(1, (BATCH, SEQ, D_MODEL), 1.0),
        'x_sample': nrm(2, (DEC_BATCH, DEC_SEQ, D_MODEL), 1.0),
        'cache_kv': nrm(3, (N_EVEN, n_pool, PAGE_SIZE, 2, ATTN_HEADS, 2 * HEAD_DIM), 1.0),
        'page_table': page_table,
        'state_s5_re': nrm(4, (N_EVEN, DEC_BATCH, S5_GROUPS, S5_STATE), 0.3),
        'state_s5_im': nrm(5, (N_EVEN, DEC_BATCH, S5_GROUPS, S5_STATE), 0.3),
        'state_lru': nrm(6, (N_ODD, DEC_BATCH, D_RNN), 0.5),
        'state_conv': nrm(7, (N_ODD, DEC_BATCH, CONV_W - 1, D_RNN), 1.0),
        'c_prompt': nrm(8, (BATCH, D_MODEL), 1.0),
        'c_sample': nrm(9, (DEC_BATCH, D_MODEL), 1.0),
        'norm_mix': 1.0 + nrm(10, (DEPTH, D_MODEL), 0.02),
        'norm_ffn': 1.0 + nrm(11, (DEPTH, D_MODEL), 0.02),
        'w_ada': nrm(12, (DEPTH, D_MODEL, 6 * D_MODEL), 0.5 * D_MODEL ** -0.5),
        'b_ada': nrm(13, (DEPTH, 6 * D_MODEL), 0.02),
        'w_in_even': nrm(14, (N_EVEN, D_MODEL, D_IN_EVEN), D_MODEL ** -0.5),
        'w_out_even': nrm(15, (N_EVEN, D_OUT_EVEN, D_MODEL), D_OUT_EVEN ** -0.5),
        'q_norm': 1.0 + nrm(16, (N_EVEN, HEAD_DIM), 0.02),
        'k_norm': 1.0 + nrm(17, (N_EVEN, HEAD_DIM), 0.02),
        'lambda_q1': nrm(18, (N_EVEN, HEAD_DIM), 0.1),
        'lambda_k1': nrm(19, (N_EVEN, HEAD_DIM), 0.1),
        'lambda_q2': nrm(20, (N_EVEN, HEAD_DIM), 0.1),
        'lambda_k2': nrm(21, (N_EVEN, HEAD_DIM), 0.1),
        'attn_subln': 1.0 + nrm(22, (N_EVEN, 2 * HEAD_DIM), 0.02),
        's5_lambda_re': -0.5 + nrm(23, (N_EVEN, S5_GROUPS, S5_STATE), 0.01),
        's5_lambda_im': jnp.pi * s5_n + nrm(24, (N_EVEN, S5_GROUPS, S5_STATE), 0.01),
        's5_b_re': nrm(25, (N_EVEN, S5_GROUPS, S5_STATE, S5_GROUP), (2 * S5_GROUP) ** -0.5),
        's5_b_im': nrm(26, (N_EVEN, S5_GROUPS, S5_STATE, S5_GROUP), (2 * S5_GROUP) ** -0.5),
        's5_c_re': nrm(27, (N_EVEN, S5_GROUPS, S5_GROUP, S5_STATE), S5_STATE ** -0.5),
        's5_c_im': nrm(28, (N_EVEN, S5_GROUPS, S5_GROUP, S5_STATE), S5_STATE ** -0.5),
        's5_d': nrm(29, (N_EVEN, S5_GROUPS, S5_GROUP), 0.5),
        's5_log_dt': jax.random.uniform(ks[30], (N_EVEN, S5_GROUPS), F32,
                                        math.log(S5_DT_MIN), math.log(S5_DT_MAX)),
        's5_w_glu': nrm(31, (N_EVEN, S5_CH, S5_CH), S5_CH ** -0.5),
        's5_b_glu': nrm(32, (N_EVEN, S5_CH), 0.02),
        'w_in_odd': nrm(33, (N_ODD, D_MODEL, 2 * D_RNN), D_MODEL ** -0.5),
        'conv_w': nrm(34, (N_ODD, CONV_W, D_RNN), CONV_W ** -0.5),
        'conv_b': nrm(35, (N_ODD, D_RNN), 0.02),
        'lru_w_a': nrm(36, (N_ODD, LRU_BLOCKS, LRU_BS, LRU_BS), LRU_BS ** -0.5),
        'lru_b_a': nrm(37, (N_ODD, D_RNN), 0.02),
        'lru_w_x': nrm(38, (N_ODD, LRU_BLOCKS, LRU_BS, LRU_BS), LRU_BS ** -0.5),
        'lru_b_x': nrm(39, (N_ODD, D_RNN), 0.02),
        'lru_lambda': jnp.log(lru_u) - jnp.log1p(-lru_u),
        'w_out_odd': nrm(41, (N_ODD, D_RNN, D_MODEL), D_RNN ** -0.5),
        'peer_wq': nrm(42, (DEPTH, D_MODEL, PEER_HEADS * PEER_DKEY), D_MODEL ** -0.5),
        'peer_keys': nrm(43, (DEPTH, PEER_HEADS, 2, PEER_NKEYS, PEER_DKEY // 2), (PEER_DKEY // 2) ** -0.5),
        'peer_u': nrm(44, (DEPTH, PEER_EXPERTS, D_MODEL), D_MODEL ** -0.5),
        'peer_v': nrm(45, (DEPTH, PEER_EXPERTS, D_MODEL), PEER_HEADS ** -0.5),
    }


def reference(x_prompt, x_sample, cache_kv, page_table, state_s5_re, state_s5_im, state_lru, state_conv,
              c_prompt, c_sample, norm_mix, norm_ffn, w_ada, b_ada, w_in_even, w_out_even,
              q_norm, k_norm, lambda_q1, lambda_k1, lambda_q2, lambda_k2, attn_subln,
              s5_lambda_re, s5_lambda_im, s5_b_re, s5_b_im, s5_c_re, s5_c_im, s5_d, s5_log_dt,
              s5_w_glu, s5_b_glu, w_in_odd, conv_w, conv_b, lru_w_a, lru_b_a, lru_w_x, lru_b_x,
              lru_lambda, w_out_odd, peer_wq, peer_keys, peer_u, peer_v):
    p = dict(norm_mix=norm_mix, norm_ffn=norm_ffn, w_ada=w_ada, b_ada=b_ada,
             w_in_even=w_in_even, w_out_even=w_out_even, q_norm=q_norm, k_norm=k_norm,
             lambda_q1=lambda_q1, lambda_k1=lambda_k1, lambda_q2=lambda_q2, lambda_k2=lambda_k2,
             attn_subln=attn_subln, s5_lambda_re=s5_lambda_re, s5_lambda_im=s5_lambda_im,
             s5_b_re=s5_b_re, s5_b_im=s5_b_im, s5_c_re=s5_c_re, s5_c_im=s5_c_im, s5_d=s5_d,
             s5_log_dt=s5_log_dt, s5_w_glu=s5_w_glu, s5_b_glu=s5_b_glu, w_in_odd=w_in_odd,
             conv_w=conv_w, conv_b=conv_b, lru_w_a=lru_w_a, lru_b_a=lru_b_a, lru_w_x=lru_w_x,
             lru_b_x=lru_b_x, lru_lambda=lru_lambda, w_out_odd=w_out_odd, peer_wq=peer_wq,
             peer_keys=peer_keys, peer_u=peer_u, peer_v=peer_v)
    nb_p = x_prompt.shape[0]
    pos_prompt = jnp.arange(x_prompt.shape[1], dtype=jnp.int32)
    z_s5 = jnp.zeros((N_EVEN, nb_p, S5_GROUPS, S5_STATE), F32)
    z_lru = jnp.zeros((N_ODD, nb_p, D_RNN), F32)
    z_conv = jnp.zeros((N_ODD, nb_p, CONV_W - 1, D_RNN), x_prompt.dtype)
    y_prompt, kv_prompt, s5_re_prompt, s5_im_prompt, lru_prompt, conv_prompt = _trunk(
        x_prompt, c_prompt, pos_prompt, None, None, z_s5, z_s5, z_lru, z_conv, p)
    past_len = page_table.shape[1] * PAGE_SIZE
    pos_sample = past_len + jnp.arange(x_sample.shape[1], dtype=jnp.int32)
    y_sample, kv_sample, s5_re_sample, s5_im_sample, lru_sample, conv_sample = _trunk(
        x_sample, c_sample, pos_sample, cache_kv, page_table, state_s5_re, state_s5_im,
        state_lru, state_conv, p)
    return (y_prompt, y_sample, kv_prompt, kv_sample, s5_re_prompt, s5_im_prompt, s5_re_sample,
            s5_im_sample, lru_prompt, lru_sample, conv_prompt, conv_sample)
```

```python
import functools
import math

import jax
import jax.numpy as jnp
from jax import lax
from jax.experimental import pallas as pl
from jax.experimental.pallas import tpu as pltpu

F32 = jnp.float32
BF16 = jnp.bfloat16
EPS = 1e-6
ROPE_THETA = 500000.0
LRU_C = 8.0
PEER_TOPK = 16
LANES = 128
SUBLANES = 8
NEG_INF = float("-inf")


def _params(sem, vmem_mb=48):
    return pltpu.CompilerParams(dimension_semantics=sem, vmem_limit_bytes=vmem_mb * 1024 * 1024)


def _gelu(x):
    return 0.5 * x * (1.0 + jnp.tanh(math.sqrt(2.0 / math.pi) * (x + 0.044715 * (x * x * x))))


def _sigmoid(x):
    return 1.0 / (1.0 + jnp.exp(-x))


def _mod_kernel(c_ref, w_ref, b_ref, o_ref):
    c = c_ref[...]
    a = (c * _sigmoid(c)).astype(BF16)
    o_ref[...] = jnp.dot(a, w_ref[...], preferred_element_type=F32) + b_ref[...]


def _mod(c_all, w, b):
    m, k = c_all.shape
    n = w.shape[1]
    tn = 1024
    return pl.pallas_call(
        _mod_kernel,
        grid=(n // tn,),
        in_specs=[pl.BlockSpec((m, k), lambda j: (0, 0)),
                  pl.BlockSpec((k, tn), lambda j: (0, j)),
                  pl.BlockSpec((1, tn), lambda j: (0, j))],
        out_specs=pl.BlockSpec((m, tn), lambda j: (0, j)),
        out_shape=jax.ShapeDtypeStruct((m, n), F32),
        compiler_params=_params(("arbitrary",)),
        name="adaln_mod",
    )(c_all, w, b.reshape(1, n))


def _norm_mod(x3, g_ref, sc_ref, sh_ref):
    ms = jnp.mean(x3 * x3, axis=-1, keepdims=True)
    y = x3 * lax.rsqrt(ms + EPS) * g_ref[...]
    return y * (1.0 + sc_ref[...]) + sh_ref[...]


def _linear_kernel(*refs, n_x, has_norm, has_resid, tm):
    it = iter(refs)
    x_refs = [next(it) for _ in range(n_x)]
    w_refs = [next(it) for _ in range(n_x)]
    if has_norm:
        g_ref, sc_ref, sh_ref = next(it), next(it), next(it)
    if has_resid:
        r_ref, gt_ref = next(it), next(it)
    o_ref = next(it)
    if has_norm:
        xn_ref = next(it)

        @pl.when(pl.program_id(1) == 0)
        def _():
            y = _norm_mod(x_refs[0][...], g_ref, sc_ref, sh_ref)
            xn_ref[...] = y.reshape(tm, y.shape[-1]).astype(BF16)

        acc = jnp.dot(xn_ref[...], w_refs[0][...], preferred_element_type=F32)
    else:
        acc = None
        for x_ref, w_ref in zip(x_refs, w_refs):
            d = jnp.dot(x_ref[...].astype(BF16), w_ref[...], preferred_element_type=F32)
            acc = d if acc is None else acc + d
    y3 = acc.reshape(tm // SUBLANES, SUBLANES, acc.shape[-1])
    if has_resid:
        y3 = r_ref[...] + gt_ref[...] * y3
    o_ref[...] = y3.astype(o_ref.dtype)


def _linear(xs, ws, *, norm=None, resid=None, out_dtype=F32, tm=512, tn=1024, name="linear"):
    has_norm, has_resid = norm is not None, resid is not None
    n = ws[0].shape[1]
    tn = min(tn, n)
    if has_norm:
        t8, _, k = xs[0].shape
        t = t8 * SUBLANES
    else:
        t = xs[0].shape[0]
    assert t % tm == 0 and n % tn == 0
    g8 = tm // SUBLANES
    in_specs, args = [], []
    for x in xs:
        if has_norm:
            in_specs.append(pl.BlockSpec((g8, SUBLANES, x.shape[-1]), lambda i, j: (i, 0, 0)))
        else:
            in_specs.append(pl.BlockSpec((tm, x.shape[-1]), lambda i, j: (i, 0)))
        args.append(x)
    for w in ws:
        in_specs.append(pl.BlockSpec((w.shape[0], tn), lambda i, j: (0, j)))
        args.append(w)
    scratch = []
    if has_norm:
        k = xs[0].shape[-1]
        in_specs += [pl.BlockSpec((1, 1, k), lambda i, j: (0, 0, 0)),
                     pl.BlockSpec((g8, 1, k), lambda i, j: (i, 0, 0)),
                     pl.BlockSpec((g8, 1, k), lambda i, j: (i, 0, 0))]
        args += list(norm)
        scratch.append(pltpu.VMEM((tm, k), BF16))
    if has_resid:
        in_specs += [pl.BlockSpec((g8, SUBLANES, tn), lambda i, j: (i, 0, j)),
                     pl.BlockSpec((g8, 1, tn), lambda i, j: (i, 0, j))]
        args += list(resid)
    return pl.pallas_call(
        functools.partial(_linear_kernel, n_x=len(xs), has_norm=has_norm, has_resid=has_resid, tm=tm),
        grid=(t // tm, n // tn),
        in_specs=in_specs,
        out_specs=pl.BlockSpec((g8, SUBLANES, tn), lambda i, j: (i, 0, j)),
        out_shape=jax.ShapeDtypeStruct((t // SUBLANES, SUBLANES, n), out_dtype),
        scratch_shapes=scratch,
        compiler_params=_params(("parallel", "arbitrary")),
        name=name,
    )(*args)


def _qkprep_kernel(q_ref, k_ref, v_ref, gq_ref, gk_ref, c_ref, s1_ref, s2_ref, p_ref, qo_ref, kv_ref, *,
                   nh, hd, q_scale):
    cos, sin_lo, sin_hi = c_ref[...], s1_ref[...], s2_ref[...]
    ones_blk = p_ref[...]
    rot = hd // 8

    def prep(x, g):
        sq = x * x
        hi = sq.astype(BF16)
        lo = (sq - hi.astype(F32)).astype(BF16)
        ss = (jnp.dot(hi, ones_blk, preferred_element_type=F32)
              + jnp.dot(lo, ones_blk, preferred_element_type=F32))
        y = x * lax.rsqrt(ss * (1.0 / hd) + EPS) * g
        return y * cos + pltpu.roll(y, LANES - rot, 1) * sin_lo + pltpu.roll(y, rot, 1) * sin_hi

    for h in range(nh):
        sl = slice(h * LANES, (h + 1) * LANES)
        qo_ref[:, sl] = (prep(q_ref[:, sl], gq_ref[...]) * q_scale).astype(qo_ref.dtype)
        kv_ref[:, sl] = prep(k_ref[:, sl], gk_ref[...])
    kv_ref[:, nh * LANES:] = v_ref[...]


def _qkprep(proj, gq, gk, cos_t, sin_lo_t, sin_hi_t, ones_blk, *, nh, hd, tm=512):
    t, n = proj.shape
    w = nh * 2 * hd
    assert 2 * hd == LANES and n == 4 * w and t % tm == 0
    row = lambda i: (i, 0)
    full = lambda i: (0, 0)
    return pl.pallas_call(
        functools.partial(_qkprep_kernel, nh=nh, hd=hd, q_scale=hd ** -0.5),
        grid=(t // tm,),
        in_specs=[pl.BlockSpec((tm, w), lambda i: (i, 1)),
                  pl.BlockSpec((tm, w), lambda i: (i, 2)),
                  pl.BlockSpec((tm, w), lambda i: (i, 3)),
                  pl.BlockSpec((1, LANES), full), pl.BlockSpec((1, LANES), full),
                  pl.BlockSpec((tm, LANES), row), pl.BlockSpec((tm, LANES), row), pl.BlockSpec((tm, LANES), row),
                  pl.BlockSpec((LANES, LANES), full)],
        out_specs=[pl.BlockSpec((tm, w), row), pl.BlockSpec((tm, 2 * w), row)],
        out_shape=[jax.ShapeDtypeStruct((t, w), BF16), jax.ShapeDtypeStruct((t, 2 * w), F32)],
        compiler_params=_params(("parallel",)),
        name="qk_prep",
    )(proj, proj, proj, gq, gk, cos_t, sin_lo_t, sin_hi_t, ones_blk)


def _subln(o, g_ref, out_scale):
    return o * lax.rsqrt(jnp.mean(o * o, axis=-1, keepdims=True) + EPS) * g_ref[...] * out_scale


def _online_softmax_step(s, v, m_ref, l_ref, acc_ref, idx):
    m_prev = m_ref[idx]
    m_new = jnp.maximum(m_prev, jnp.max(s, axis=-1, keepdims=True))
    alpha = jnp.exp(m_prev - m_new)
    p = jnp.exp(s - m_new)
    l_ref[idx] = alpha * l_ref[idx] + jnp.sum(p, axis=-1, keepdims=True)
    acc_ref[idx] = alpha * acc_ref[idx] + jnp.dot(p.astype(BF16), v, preferred_element_type=F32)
    m_ref[idx] = m_new


def _flash_kernel(lam_ref, q_ref, k_ref, v_ref, g_ref, o_ref, m_ref, l_ref, acc_ref, *, tq, hd, out_scale):
    qi, ki = pl.program_id(2), pl.program_id(3)

    @pl.when(ki == 0)
    def _():
        m_ref[...] = jnp.full(m_ref.shape, NEG_INF, F32)
        l_ref[...] = jnp.zeros(l_ref.shape, F32)
        acc_ref[...] = jnp.zeros(acc_ref.shape, F32)

    def step(masked):
        q = q_ref[0]
        k = k_ref[0].astype(BF16)
        v = v_ref[0].astype(BF16)
        lane = lax.broadcasted_iota(jnp.int32, q.shape, 1)
        for s in range(2):
            qs = jnp.where((lane >= s * hd) == (lane < (s + 1) * hd), q, jnp.zeros_like(q))
            sc = lax.dot_general(qs, k, (((1,), (1,)), ((), ())), preferred_element_type=F32)
            if masked:
                row = lax.broadcasted_iota(jnp.int32, sc.shape, 0)
                col = lax.broadcasted_iota(jnp.int32, sc.shape, 1)
                sc = jnp.where(col <= row, sc, NEG_INF)
            _online_softmax_step(sc, v, m_ref, l_ref, acc_ref, s)

    @pl.when(ki < qi)
    def _():
        step(False)

    @pl.when(ki == qi)
    def _():
        step(True)
        o = acc_ref[0] / l_ref[0] - lam_ref[...] * (acc_ref[1] / l_ref[1])
        o_ref[0] = _subln(o, g_ref, out_scale).astype(o_ref.dtype)


def _flash_attention(q3, kv3, lam, subln_g, *, nh, hd, out_scale, tq=512):
    b, l, w = q3.shape
    assert l % tq == 0
    nq = l // tq
    return pl.pallas_call(
        functools.partial(_flash_kernel, tq=tq, hd=hd, out_scale=out_scale),
        grid=(b, nh, nq, nq),
        in_specs=[pl.BlockSpec((1, 1), lambda bb, h, qi, ki: (0, 0)),
                  pl.BlockSpec((1, tq, LANES), lambda bb, h, qi, ki: (bb, qi, h)),
                  pl.BlockSpec((1, tq, LANES), lambda bb, h, qi, ki: (bb, jnp.minimum(ki, qi), h)),
                  pl.BlockSpec((1, tq, LANES), lambda bb, h, qi, ki: (bb, jnp.minimum(ki, qi), nh + h)),
                  pl.BlockSpec((1, LANES), lambda bb, h, qi, ki: (0, 0))],
        out_specs=pl.BlockSpec((1, tq, LANES), lambda bb, h, qi, ki: (bb, qi, h)),
        out_shape=jax.ShapeDtypeStruct((b, l, w), BF16),
        scratch_shapes=[pltpu.VMEM((2, tq, 1), F32), pltpu.VMEM((2, tq, 1), F32), pltpu.VMEM((2, tq, LANES), F32)],
        compiler_params=_params(("parallel", "parallel", "parallel", "arbitrary")),
        name="diff_attn_prompt",
    )(lam, q3, kv3, kv3, subln_g)


def _decode_kernel(pt_ref, lam_ref, q_ref, kvn_ref, *rest, npg, nh, hd, dl, out_scale):
    page_refs = rest[:npg]
    g_ref, o_ref, m_ref, l_ref, acc_ref, qx_ref = rest[npg:]
    p = pl.program_id(1)
    w = nh * 2 * hd
    rows = nh * 2 * dl

    @pl.when(p == 0)
    def _():
        m_ref[...] = jnp.full(m_ref.shape, NEG_INF, F32)
        l_ref[...] = jnp.zeros(l_ref.shape, F32)
        acc_ref[...] = jnp.zeros(acc_ref.shape, F32)
        q = q_ref[0].astype(F32)
        qt = jnp.concatenate([q] * (nh * 2), axis=0)
        r = lax.broadcasted_iota(jnp.int32, (rows, w), 0)
        c = lax.broadcasted_iota(jnp.int32, (rows, w), 1)
        qx_ref[...] = jnp.where(c // hd == r // dl, qt, 0.0).astype(BF16)

    def attend(kv, mask):
        k = kv[:, :w].astype(BF16)
        v = kv[:, w:].astype(BF16)
        s = lax.dot_general(qx_ref[...], k, (((1,), (1,)), ((), ())), preferred_element_type=F32)
        if mask is not None:
            s = jnp.where(mask, s, NEG_INF)
        _online_softmax_step(s, v, m_ref, l_ref, acc_ref, 0)

    for ref in page_refs:
        attend(ref[...], None)

    @pl.when(p == pl.num_programs(1) - 1)
    def _():
        nk = page_refs[0].shape[0]
        kvn = jnp.concatenate([kvn_ref[0], jnp.zeros((nk - dl, 2 * w), F32)], axis=0)
        r = lax.broadcasted_iota(jnp.int32, (rows, nk), 0)
        c = lax.broadcasted_iota(jnp.int32, (rows, nk), 1)
        attend(kvn, c <= r % dl)
        acc = acc_ref[0] / l_ref[0]
        for h in range(nh):
            r1 = slice((2 * h) * dl, (2 * h + 1) * dl)
            r2 = slice((2 * h + 1) * dl, (2 * h + 2) * dl)
            cs = slice(h * LANES, (h + 1) * LANES)
            o = acc[r1, cs] - lam_ref[...] * acc[r2, cs]
            o_ref[0, :, cs] = _subln(o, g_ref, out_scale).astype(o_ref.dtype)


def _decode_attention(q3, kvn3, cache3, page_table, lam, subln_g, *, nh, hd, out_scale, npg=4):
    db, dl, w = q3.shape
    n_pages = page_table.shape[1]
    psz = cache3.shape[1]
    assert n_pages % npg == 0 and dl <= psz and dl == SUBLANES
    rows = nh * 2 * dl

    def page_spec(i):
        return pl.BlockSpec((pl.Squeezed(), psz, 2 * w), lambda b, p, pt: (pt[b, p * npg + i], 0, 0))

    grid_spec = pltpu.PrefetchScalarGridSpec(
        num_scalar_prefetch=1,
        grid=(db, n_pages // npg),
        in_specs=[pl.BlockSpec((1, 1), lambda b, p, pt: (0, 0)),
                  pl.BlockSpec((1, dl, w), lambda b, p, pt: (b, 0, 0)),
                  pl.BlockSpec((1, dl, 2 * w), lambda b, p, pt: (b, 0, 0))]
        + [page_spec(i) for i in range(npg)]
        + [pl.BlockSpec((1, LANES), lambda b, p, pt: (0, 0))],
        out_specs=pl.BlockSpec((1, dl, w), lambda b, p, pt: (b, 0, 0)),
        scratch_shapes=[pltpu.VMEM((1, rows, 1), F32), pltpu.VMEM((1, rows, 1), F32),
                        pltpu.VMEM((1, rows, w), F32), pltpu.VMEM((rows, w), BF16)],
    )
    return pl.pallas_call(
        functools.partial(_decode_kernel, npg=npg, nh=nh, hd=hd, dl=dl, out_scale=out_scale),
        grid_spec=grid_spec,
        out_shape=jax.ShapeDtypeStruct((db, dl, w), BF16),
        compiler_params=_params(("parallel", "arbitrary")),
        name="diff_attn_sample",
    )(page_table, lam, q3, kvn3, *([cache3] * npg), subln_g)


def _s5_kernel(*refs, n_u, nseq, ts, lc, npk, out3d):
    u_refs = refs[:n_u]
    (wb_ref, wc_ref, d_ref, ar_ref, ai_ref, h0r_ref, h0i_ref, wg_ref, bg_ref,
     y_ref, hr_out, hi_out, xr_s, xi_s, hst_s) = refs[n_u:]
    s = pl.program_id(1)
    nlb = xr_s.shape[0]
    nst = nlb * LANES
    pb = nlb // npk
    pw = pb * LANES
    cw = d_ref.shape[1] // npk
    gb = lc // LANES

    @pl.when(s == 0)
    def _():
        hst_s[0] = h0r_ref[0]
        hst_s[1] = h0i_ref[0]

    u = jnp.concatenate([r[...] for r in u_refs], axis=0) if n_u > 1 else u_refs[0][...]
    ub = u.astype(BF16)
    for pk in range(npk):
        x = jnp.dot(ub[:, pk * cw:(pk + 1) * cw], wb_ref[pk], preferred_element_type=F32)
        for q in range(pb):
            xr_s[pk * pb + q] = x[:, q * LANES:(q + 1) * LANES]
            xi_s[pk * pb + q] = x[:, pw + q * LANES:pw + (q + 1) * LANES]

    for c in range(nlb // gb):
        blks = list(range(c * gb, (c + 1) * gb))
        ar = [jnp.broadcast_to(ar_ref[:, q * LANES:(q + 1) * LANES], (nseq, LANES)) for q in blks]
        ai = [jnp.broadcast_to(ai_ref[:, q * LANES:(q + 1) * LANES], (nseq, LANES)) for q in blks]

        def body(t, carry, blks=blks, ar=ar, ai=ai):
            idx = pl.ds(t, nseq, stride=ts)
            out = []
            for n, q in enumerate(blks):
                hr, hi = carry[2 * n], carry[2 * n + 1]
                nr = ar[n] * hr - ai[n] * hi + xr_s[q, idx, :]
                ni = ar[n] * hi + ai[n] * hr + xi_s[q, idx, :]
                xr_s[q, idx, :] = nr
                xi_s[q, idx, :] = ni
                out += [nr, ni]
            return tuple(out)

        init = []
        for q in blks:
            init += [hst_s[0, :, q * LANES:(q + 1) * LANES], hst_s[1, :, q * LANES:(q + 1) * LANES]]
        fin = lax.fori_loop(0, ts, body, tuple(init))
        for n, q in enumerate(blks):
            hst_s[0, :, q * LANES:(q + 1) * LANES] = fin[2 * n]
            hst_s[1, :, q * LANES:(q + 1) * LANES] = fin[2 * n + 1]

    ys = []
    for pk in range(npk):
        hr_b = jnp.concatenate([xr_s[pk * pb + q] for q in range(pb)], axis=1).astype(BF16)
        hi_b = jnp.concatenate([xi_s[pk * pb + q] for q in range(pb)], axis=1).astype(BF16)
        ys.append(jnp.dot(hr_b, wc_ref[0, pk], preferred_element_type=F32)
                  + jnp.dot(hi_b, wc_ref[1, pk], preferred_element_type=F32))
    y = jnp.concatenate(ys, axis=1) + d_ref[...] * u
    g = _gelu(y)
    out = g * _sigmoid(jnp.dot(g.astype(BF16), wg_ref[...], preferred_element_type=F32) + bg_ref[...])
    if out3d:
        y_ref[...] = out.reshape(y_ref.shape).astype(y_ref.dtype)
    else:
        y_ref[...] = out.astype(y_ref.dtype)

    @pl.when(s == pl.num_programs(1) - 1)
    def _():
        hr_out[0] = hst_s[0]
        hi_out[0] = hst_s[1]


def _s5(proj, h0r, h0i, wts, *, row0, nchunk, nseq, slen, ts, prompt):
    wb, wc, dd, ar, ai, wg, bg = wts
    npk = wb.shape[0]
    ch = dd.shape[1]
    nst = ar.shape[1]
    nsteps = slen // ts
    rows = nseq * ts
    lc = max(LANES, 512 // (-(-nseq // SUBLANES)))
    if prompt:
        assert nchunk == 1 and row0 == 0
        u_specs = [pl.BlockSpec((ts, ch), lambda c, s, q=q: (q * nsteps + s, 0)) for q in range(nseq)]
        y_spec = pl.BlockSpec((nseq, ts, ch), lambda c, s: (0, s, 0))
        y_shape = jax.ShapeDtypeStruct((nseq, slen, ch), BF16)
    else:
        assert nsteps == 1 and row0 % rows == 0
        u_specs = [pl.BlockSpec((rows, ch), lambda c, s: (row0 // rows + c, 0))]
        y_spec = pl.BlockSpec((rows, ch), lambda c, s: (c, 0))
        y_shape = jax.ShapeDtypeStruct((nchunk * rows, ch), BF16)
    n_u = len(u_specs)
    full2 = lambda c, s: (0, 0)
    st_spec = pl.BlockSpec((1, nseq, nst), lambda c, s: (c, 0, 0))
    return pl.pallas_call(
        functools.partial(_s5_kernel, n_u=n_u, nseq=nseq, ts=ts, lc=lc, npk=npk, out3d=prompt),
        grid=(nchunk, nsteps),
        in_specs=u_specs + [
            pl.BlockSpec(wb.shape, lambda c, s: (0, 0, 0)),
            pl.BlockSpec(wc.shape, lambda c, s: (0, 0, 0, 0)),
            pl.BlockSpec((1, ch), full2), pl.BlockSpec((1, nst), full2), pl.BlockSpec((1, nst), full2),
            st_spec, st_spec,
            pl.BlockSpec(wg.shape, full2), pl.BlockSpec((1, ch), full2)],
        out_specs=[y_spec, st_spec, st_spec],
        out_shape=[y_shape, jax.ShapeDtypeStruct((nchunk, nseq, nst), F32),
                   jax.ShapeDtypeStruct((nchunk, nseq, nst), F32)],
        scratch_shapes=[pltpu.VMEM((nst // LANES, rows, LANES), F32), pltpu.VMEM((nst // LANES, rows, LANES), F32),
                        pltpu.VMEM((2, nseq, nst), F32)],
        compiler_params=_params(("parallel", "arbitrary")),
        name="s5_prompt" if prompt else "s5_sample",
    )(*([proj] * n_u), wb, wc, dd, ar, ai, h0r, h0i, wg, bg)


def _lru_kernel(*refs, n_x, nseq, ts, lc, nblk, out3d):
    g_refs = refs[:n_x]
    x_refs = refs[n_x:2 * n_x]
    (cw_ref, cb_ref, wa_ref, ba_ref, wx_ref, bx_ref, sp_ref, buf_ref, h0_ref,
     y_ref, buf_out, h_out, a_s, b_s, prev_s, h_s) = refs[2 * n_x:]
    s = pl.program_id(1)
    dr = cb_ref.shape[1]
    bs = dr // nblk
    rows = nseq * ts

    @pl.when(s == 0)
    def _():
        prev_s[...] = buf_ref[0]
        h_s[...] = h0_ref[0]

    if n_x > 1:
        x3 = jnp.stack([r[...] for r in x_refs], axis=0)
        gate = jnp.concatenate([r[...] for r in g_refs], axis=0)
    else:
        x3 = x_refs[0][...].reshape(nseq, ts, dr)
        gate = g_refs[0][...]
    xx = jnp.concatenate([prev_s[...], x3], axis=1)
    kw = cw_ref.shape[0]
    conv = cb_ref[...] + xx[:, SUBLANES - (kw - 1):SUBLANES - (kw - 1) + ts] * cw_ref[0:1, :]
    for tap in range(1, kw):
        off = SUBLANES - (kw - 1) + tap
        conv = conv + xx[:, off:off + ts] * cw_ref[tap:tap + 1, :]
    prev_s[...] = xx[:, ts:ts + SUBLANES]

    c2 = conv.reshape(rows, dr)
    cb16 = c2.astype(BF16)
    rs, is_ = [], []
    for n in range(nblk):
        blk = cb16[:, n * bs:(n + 1) * bs]
        rs.append(jnp.dot(blk, wa_ref[n], preferred_element_type=F32))
        is_.append(jnp.dot(blk, wx_ref[n], preferred_element_type=F32))
    r = _sigmoid(jnp.concatenate(rs, axis=1) + ba_ref[...])
    i = _sigmoid(jnp.concatenate(is_, axis=1) + bx_ref[...])
    log_a = -LRU_C * r * sp_ref[...]
    a_all = jnp.exp(log_a)
    b_all = jnp.sqrt(1.0 - jnp.exp(2.0 * log_a)) * (i * c2)
    nlb = dr // LANES
    for q in range(nlb):
        a_s[q] = a_all[:, q * LANES:(q + 1) * LANES]
        b_s[q] = b_all[:, q * LANES:(q + 1) * LANES]

    gb = lc // LANES
    for c in range(nlb // gb):
        blks = list(range(c * gb, (c + 1) * gb))

        def body(t, carry, blks=blks):
            idx = pl.ds(t, nseq, stride=ts)
            out = []
            for n, q in enumerate(blks):
                h = a_s[q, idx, :] * carry[n] + b_s[q, idx, :]
                b_s[q, idx, :] = h
                out.append(h)
            return tuple(out)

        fin = lax.fori_loop(0, ts, body, tuple(h_s[:, q * LANES:(q + 1) * LANES] for q in blks))
        for n, q in enumerate(blks):
            h_s[:, q * LANES:(q + 1) * LANES] = fin[n]

    out = _gelu(gate) * jnp.concatenate([b_s[q] for q in range(nlb)], axis=1)
    if out3d:
        y_ref[...] = out.reshape(y_ref.shape).astype(y_ref.dtype)
    else:
        y_ref[...] = out.astype(y_ref.dtype)

    @pl.when(s == pl.num_programs(1) - 1)
    def _():
        buf_out[0] = prev_s[...]
        h_out[0] = h_s[...]


def _lru(proj, buf8, h0, wts, *, row0, nchunk, nseq, slen, ts, prompt):
    cw, cb, wa, ba, wx, bx, sp = wts
    dr = cb.shape[1]
    nblk = wa.shape[0]
    nsteps = slen // ts
    rows = nseq * ts
    lc = dr if nseq <= SUBLANES else 256
    assert dr % lc == 0
    if prompt:
        assert nchunk == 1 and row0 == 0
        g_specs = [pl.BlockSpec((ts, dr), lambda c, s, q=q: (q * nsteps + s, 0)) for q in range(nseq)]
        x_specs = [pl.BlockSpec((ts, dr), lambda c, s, q=q: (q * nsteps + s, 1)) for q in range(nseq)]
        y_spec = pl.BlockSpec((nseq, ts, dr), lambda c, s: (0, s, 0))
        y_shape = jax.ShapeDtypeStruct((nseq, slen, dr), BF16)
    else:
        assert nsteps == 1 and row0 % rows == 0 and ts == SUBLANES
        g_specs = [pl.BlockSpec((rows, dr), lambda c, s: (row0 // rows + c, 0))]
        x_specs = [pl.BlockSpec((rows, dr), lambda c, s: (row0 // rows + c, 1))]
        y_spec = pl.BlockSpec((rows, dr), lambda c, s: (c, 0))
        y_shape = jax.ShapeDtypeStruct((nchunk * rows, dr), BF16)
    n_x = len(x_specs)
    full2 = lambda c, s: (0, 0)
    full3 = lambda c, s: (0, 0, 0)
    return pl.pallas_call(
        functools.partial(_lru_kernel, n_x=n_x, nseq=nseq, ts=ts, lc=lc, nblk=nblk, out3d=prompt),
        grid=(nchunk, nsteps),
        in_specs=g_specs + x_specs + [
            pl.BlockSpec(cw.shape, full2), pl.BlockSpec((1, dr), full2),
            pl.BlockSpec(wa.shape, full3), pl.BlockSpec((1, dr), full2),
            pl.BlockSpec(wx.shape, full3), pl.BlockSpec((1, dr), full2),
            pl.BlockSpec((1, dr), full2),
            pl.BlockSpec((1, nseq, SUBLANES, dr), lambda c, s: (c, 0, 0, 0)),
            pl.BlockSpec((1, nseq, dr), lambda c, s: (c, 0, 0))],
        out_specs=[y_spec,
                   pl.BlockSpec((1, nseq, SUBLANES, dr), lambda c, s: (c, 0, 0, 0)),
                   pl.BlockSpec((1, nseq, dr), lambda c, s: (c, 0, 0))],
        out_shape=[y_shape, jax.ShapeDtypeStruct((nchunk, nseq, SUBLANES, dr), F32),
                   jax.ShapeDtypeStruct((nchunk, nseq, dr), F32)],
        scratch_shapes=[pltpu.VMEM((dr // LANES, rows, LANES), F32), pltpu.VMEM((dr // LANES, rows, LANES), F32),
                        pltpu.VMEM((nseq, SUBLANES, dr), F32), pltpu.VMEM((nseq, dr), F32)],
        compiler_params=_params(("parallel", "arbitrary")),
        name="rglru_prompt" if prompt else "rglru_sample",
    )(*([proj] * (2 * n_x)), cw, cb, wa, ba, wx, bx, sp, buf8, h0)


def _peer_scores_kernel(x_ref, g_ref, sc_ref, sh_ref, wq_ref, k1_ref, k2_ref, hf_ref, s1_ref, s2_ref, *,
                        tm, nh, dk):
    y = _norm_mod(x_ref[...], g_ref, sc_ref, sh_ref)
    hf = y.reshape(tm, y.shape[-1]).astype(BF16)
    hf_ref[...] = hf
    q = jnp.dot(hf, wq_ref[...], preferred_element_type=F32)
    half = dk // 2
    q1, q2 = [], []
    for h in range(nh):
        qh = q[:, h * dk:(h + 1) * dk]
        qn = qh * lax.rsqrt(jnp.mean(qh * qh, axis=-1, keepdims=True) + EPS)
        q1.append(qn[:, :half])
        q2.append(qn[:, half:])
    nt = (((1,), (1,)), ((), ()))
    s1_ref[...] = lax.dot_general(k1_ref[...], jnp.concatenate(q1, axis=1).astype(BF16), nt,
                                  preferred_element_type=F32)
    s2_ref[...] = lax.dot_general(k2_ref[...], jnp.concatenate(q2, axis=1).astype(BF16), nt,
                                  preferred_element_type=F32)


def _peer_scores(x3, norm, wq, kexp1, kexp2, *, nh, tm=512):
    t8, _, d = x3.shape
    t = t8 * SUBLANES
    assert t % tm == 0
    g8 = tm // SUBLANES
    nq = wq.shape[1]
    nrow = kexp1.shape[0]
    full2 = lambda i: (0, 0)
    return pl.pallas_call(
        functools.partial(_peer_scores_kernel, tm=tm, nh=nh, dk=nq // nh),
        grid=(t // tm,),
        in_specs=[pl.BlockSpec((g8, SUBLANES, d), lambda i: (i, 0, 0)),
                  pl.BlockSpec((1, 1, d), lambda i: (0, 0, 0)),
                  pl.BlockSpec((g8, 1, d), lambda i: (i, 0, 0)),
                  pl.BlockSpec((g8, 1, d), lambda i: (i, 0, 0)),
                  pl.BlockSpec(wq.shape, full2), pl.BlockSpec(kexp1.shape, full2), pl.BlockSpec(kexp2.shape, full2)],
        out_specs=[pl.BlockSpec((tm, d), lambda i: (i, 0)),
                   pl.BlockSpec((nrow, tm), lambda i: (0, i)),
                   pl.BlockSpec((nrow, tm), lambda i: (0, i))],
        out_shape=[jax.ShapeDtypeStruct((t, d), BF16), jax.ShapeDtypeStruct((nrow, t), F32),
                   jax.ShapeDtypeStruct((nrow, t), F32)],
        compiler_params=_params(("parallel",)),
        name="peer_scores",
    )(x3, *norm, wq, kexp1, kexp2)


def _hyperbola(k):
    return [(a, b) for a in range(k) for b in range(k) if (a + 1) * (b + 1) <= k]


def _peer_topk_kernel(s1_ref, s2_ref, e1_ref, e2_ref, g_ref, x_s, v_s, i_s, *, nkeys, nh, topk):
    tl = s1_ref.shape[1]
    kid = lax.broadcasted_iota(jnp.int32, (nkeys, nh, tl), 0)

    for side, s_ref in enumerate((s1_ref, s2_ref)):
        x_s[...] = s_ref[...].reshape(nkeys, nh, tl)

        def body(it, carry, side=side):
            x = x_s[...]
            m = jnp.max(x, axis=0)
            idx = jnp.min(jnp.where(x == m[None], kid, nkeys), axis=0)
            v_s[side, it] = m
            i_s[side, it] = idx
            x_s[...] = jnp.where(kid == idx[None], NEG_INF, x)
            return carry

        lax.fori_loop(0, topk, body, 0)

    pairs = _hyperbola(topk)
    v1 = [v_s[0, k] for k in range(topk)]
    v2 = [v_s[1, k] for k in range(topk)]
    cand = [v1[a] + v2[b] for a, b in pairs]
    flat = [a * topk + b for a, b in pairs]
    big = topk * topk
    shift = topk.bit_length() - 1
    assert 1 << shift == topk
    tops, k1s, k2s = [], [], []
    for _ in range(topk):
        m = functools.reduce(jnp.maximum, cand)
        sel = functools.reduce(jnp.minimum, [jnp.where(c == m, f, big) for c, f in zip(cand, flat)])
        cand = [jnp.where(sel == f, NEG_INF, c) for c, f in zip(cand, flat)]
        tops.append(m)
        k1s.append(lax.shift_right_logical(sel, shift))
        k2s.append(sel & (topk - 1))
    i1 = [i_s[0, k] for k in range(topk)]
    i2 = [i_s[1, k] for k in range(topk)]
    ex = [jnp.exp(tv - tops[0]) for tv in tops]
    den = functools.reduce(jnp.add, ex)
    e1, e2, gs = [], [], []
    for k in range(topk):
        e1.append(functools.reduce(jnp.add, [jnp.where(k1s[k] == j, i1[j], 0) for j in range(topk)]))
        e2.append(functools.reduce(jnp.add, [jnp.where(k2s[k] == j, i2[j], 0) for j in range(topk)]))
        gs.append(ex[k] / den)
    e1_ref[...] = jnp.concatenate(e1, axis=0).astype(F32).T
    e2_ref[...] = jnp.concatenate(e2, axis=0).astype(F32).T
    g_ref[...] = jnp.concatenate(gs, axis=0).T


def _peer_topk(s1, s2, *, nkeys, nh, topk=PEER_TOPK, tl=LANES):
    nrow, t = s1.shape
    assert nrow == nkeys * nh and t % tl == 0 and nh * topk == LANES
    in_spec = pl.BlockSpec((nrow, tl), lambda i: (0, i))
    out_spec = pl.BlockSpec((tl, nh * topk), lambda i: (i, 0))
    out = jax.ShapeDtypeStruct((t, nh * topk), F32)
    return pl.pallas_call(
        functools.partial(_peer_topk_kernel, nkeys=nkeys, nh=nh, topk=topk),
        grid=(t // tl,),
        in_specs=[in_spec, in_spec],
        out_specs=[out_spec, out_spec, out_spec],
        out_shape=[out, out, out],
        scratch_shapes=[pltpu.VMEM((nkeys, nh, tl), F32), pltpu.VMEM((2, topk, nh, tl), F32),
                        pltpu.VMEM((2, topk, nh, tl), jnp.int32)],
        compiler_params=_params(("parallel",)),
        name="peer_topk",
    )(s1, s2)


def _peer_dense_kernel(hf_ref, e1_ref, e2_ref, g_ref, u_ref, v_ref, r_ref, gt_ref, o_ref, g3_s, acc_s, *,
                       tb, nkeys, eb1):
    j = pl.program_id(1)

    @pl.when(j == 0)
    def _():
        acc_s[...] = jnp.zeros(acc_s.shape, F32)
        kio = lax.broadcasted_iota(jnp.int32, (nkeys, LANES), 0).astype(F32)
        nt = (((1,), (1,)), ((), ()))

        def build(t, carry):
            e1 = jnp.broadcast_to(e1_ref[pl.ds(t, 1), :], (nkeys, LANES))
            e2 = jnp.broadcast_to(e2_ref[pl.ds(t, 1), :], (nkeys, LANES))
            gg = jnp.broadcast_to(g_ref[pl.ds(t, 1), :], (nkeys, LANES))
            a = jnp.where(kio == e1, gg, 0.0)
            a_hi = a.astype(BF16)
            a_lo = (a - a_hi.astype(F32)).astype(BF16)
            b = jnp.where(kio == e2, 1.0, 0.0).astype(BF16)
            gmat = (lax.dot_general(a_hi, b, nt, preferred_element_type=F32)
                    + lax.dot_general(a_lo, b, nt, preferred_element_type=F32))
            g3_s[pl.ds(pl.multiple_of(t * nkeys, nkeys), nkeys), :] = gmat
            return carry

        lax.fori_loop(0, tb, build, 0)

    z = lax.dot_general(hf_ref[...], u_ref[...], (((1,), (1,)), ((), ())), preferred_element_type=F32)
    act = _gelu(z)
    gate = jnp.concatenate(
        [g3_s[pl.ds(j * eb1 + a, tb, stride=nkeys), :] for a in range(eb1)], axis=1)
    acc_s[...] += jnp.dot((gate * act).astype(BF16), v_ref[...], preferred_element_type=F32)

    @pl.when(j == pl.num_programs(1) - 1)
    def _():
        o_ref[...] = r_ref[...] + gt_ref[...] * acc_s[...].reshape(o_ref.shape)


def _peer_dense(hf, e1, e2, g, u_tab, v_tab, resid, gate, *, nkeys, tb=256, eb1=8):
    t, d = hf.shape
    ne = u_tab.shape[0]
    assert ne == nkeys * nkeys and t % tb == 0 and nkeys % eb1 == 0
    eb = eb1 * nkeys
    g8 = tb // SUBLANES
    return pl.pallas_call(
        functools.partial(_peer_dense_kernel, tb=tb, nkeys=nkeys, eb1=eb1),
        grid=(t // tb, ne // eb),
        in_specs=[pl.BlockSpec((tb, d), lambda i, j: (i, 0)),
                  pl.BlockSpec((tb, LANES), lambda i, j: (i, 0)),
                  pl.BlockSpec((tb, LANES), lambda i, j: (i, 0)),
                  pl.BlockSpec((tb, LANES), lambda i, j: (i, 0)),
                  pl.BlockSpec((eb, d), lambda i, j: (j, 0)),
                  pl.BlockSpec((eb, d), lambda i, j: (j, 0)),
                  pl.BlockSpec((g8, SUBLANES, d), lambda i, j: (i, 0, 0)),
                  pl.BlockSpec((g8, 1, d), lambda i, j: (i, 0, 0))],
        out_specs=pl.BlockSpec((g8, SUBLANES, d), lambda i, j: (i, 0, 0)),
        out_shape=jax.ShapeDtypeStruct((t // SUBLANES, SUBLANES, d), F32),
        scratch_shapes=[pltpu.VMEM((tb * nkeys, LANES), F32), pltpu.VMEM((tb, d), F32)],
        compiler_params=_params(("parallel", "arbitrary"), vmem_mb=56),
        name="peer_dense",
    )(hf, e1, e2, g, u_tab, v_tab, resid, gate)


def _s5_weights(lam_re, lam_im, b_re, b_im, c_re, c_im, d, log_dt, w_glu, b_glu, *, gpp=8):
    ng, ns = lam_re.shape
    p = d.shape[1]
    dt = jnp.exp(log_dt)[:, None]
    mag = jnp.exp(lam_re * dt)
    ab_re, ab_im = mag * jnp.cos(lam_im * dt), mag * jnp.sin(lam_im * dt)
    den = lam_re * lam_re + lam_im * lam_im
    fr = ((ab_re - 1.0) * lam_re + ab_im * lam_im) / den
    fi = (ab_im * lam_re - (ab_re - 1.0) * lam_im) / den
    xb_re = fr[..., None] * b_re - fi[..., None] * b_im
    xb_im = fr[..., None] * b_im + fi[..., None] * b_re
    npk = ng // gpp
    eye = jnp.eye(gpp, dtype=F32)

    def pack_b(w):
        w = w.reshape(npk, gpp, ns, p)
        return jnp.einsum("kgnp,gh->kgphn", w, eye).reshape(npk, gpp * p, gpp * ns)

    def pack_c(w):
        w = w.reshape(npk, gpp, p, ns)
        return jnp.einsum("kgpn,gh->kgnhp", w, eye).reshape(npk, gpp * ns, gpp * p)

    wb = jnp.concatenate([pack_b(xb_re), pack_b(xb_im)], axis=2).astype(BF16)
    wc = jnp.stack([pack_c(c_re), -pack_c(c_im)], axis=0).astype(BF16)
    return (wb, wc, d.reshape(1, ng * p), ab_re.reshape(1, ng * ns), ab_im.reshape(1, ng * ns),
            w_glu.astype(BF16), b_glu.reshape(1, -1))


def _rope_tables(pos, hd):
    rot = hd // 4
    inv = jnp.power(ROPE_THETA, -jnp.arange(0, rot, 2, dtype=F32) / rot)
    ang = pos.astype(F32)[:, None] * inv[None, :]
    cos, sin = jnp.cos(ang), jnp.sin(ang)
    n = pos.shape[0]
    half = rot // 2
    ones = jnp.ones((n, hd - rot), F32)
    zeros = jnp.zeros((n, hd - rot), F32)
    zh = jnp.zeros((n, half), F32)
    cos_t = jnp.concatenate([cos, cos, ones], axis=1)
    sin_lo = jnp.concatenate([-sin, zh, zeros], axis=1)
    sin_hi = jnp.concatenate([zh, sin, zeros], axis=1)
    rep = LANES // hd
    return jnp.tile(cos_t, (1, rep)), jnp.tile(sin_lo, (1, rep)), jnp.tile(sin_hi, (1, rep))


def kernel(x_prompt, x_sample, cache_kv, page_table, state_s5_re, state_s5_im, state_lru, state_conv, c_prompt, c_sample, norm_mix, norm_ffn, w_ada, b_ada, w_in_even, w_out_even, q_norm, k_norm, lambda_q1, lambda_k1, lambda_q2, lambda_k2, attn_subln, s5_lambda_re, s5_lambda_im, s5_b_re, s5_b_im, s5_c_re, s5_c_im, s5_d, s5_log_dt, s5_w_glu, s5_b_glu, w_in_odd, conv_w, conv_b, lru_w_a, lru_b_a, lru_w_x, lru_b_x, lru_lambda, w_out_odd, peer_wq, peer_keys, peer_u, peer_v):
    nb, sl, dm = x_prompt.shape
    db, dl, _ = x_sample.shape
    depth = norm_mix.shape[0]
    n_even = w_in_even.shape[0]
    nh = cache_kv.shape[4]
    hd = cache_kv.shape[5] // 2
    psz = cache_kv.shape[2]
    n_pool = cache_kv.shape[1]
    qkv_w = nh * 2 * hd
    s5_ch = w_in_even.shape[2] - 3 * qkv_w
    ng, ns = s5_lambda_re.shape[1], s5_lambda_re.shape[2]
    dr = conv_b.shape[1]
    kw = conv_w.shape[1]
    p_heads, _, nkeys, _ = peer_keys.shape[1:]
    tp, tsmp = nb * sl, db * dl
    t = tp + tsmp
    t8 = t // SUBLANES
    past_len = page_table.shape[1] * psz

    x = jnp.concatenate([x_prompt.reshape(tp, dm), x_sample.reshape(tsmp, dm)], axis=0).reshape(t8, SUBLANES, dm)
    c_all = jnp.concatenate([c_prompt, c_sample], axis=0)

    def expand(m):
        return jnp.concatenate([jnp.repeat(m[:nb], sl // SUBLANES, axis=0),
                                jnp.repeat(m[nb:], dl // SUBLANES, axis=0)], axis=0)[:, None, :]

    pos = jnp.concatenate([jnp.tile(jnp.arange(sl, dtype=jnp.int32), nb),
                           jnp.tile(past_len + jnp.arange(dl, dtype=jnp.int32), db)])
    cos_t, sin_lo, sin_hi = _rope_tables(pos, hd)
    lane = jnp.arange(LANES)
    ones_blk = (lane[:, None] // hd == lane[None, :] // hd).astype(BF16)
    eye_h = jnp.eye(p_heads, dtype=F32)
    cache4 = cache_kv.reshape(n_even, n_pool, psz, 2 * qkv_w)
    rb = 32 if db % 32 == 0 else db
    ts_p = 64 if sl % 64 == 0 else sl

    kv_rows, s5_re, s5_im, lru_h, conv_bufs = [], [], [], [], []
    for layer in range(depth):
        j = layer // 2
        mod = _mod(c_all, w_ada[layer].astype(BF16), b_ada[layer])
        sh1, sc1, g1, sh2, sc2, g2 = [expand(m) for m in jnp.split(mod, 6, axis=-1)]
        nm = (norm_mix[layer].reshape(1, 1, dm), sc1, sh1)
        if layer % 2 == 0:
            proj = _linear([x], [w_in_even[j].astype(BF16)], norm=nm, name="in_proj_even").reshape(t, -1)
            tile2 = lambda g: jnp.tile(g, LANES // hd).reshape(1, LANES)
            qn, kv = _qkprep(proj, tile2(q_norm[j]), tile2(k_norm[j]), cos_t, sin_lo, sin_hi, ones_blk, nh=nh, hd=hd)
            kv_rows.append(kv)
            lam_init = 0.8 - 0.6 * math.exp(-0.3 * layer)
            lam = (jnp.exp(jnp.sum(lambda_q1[j] * lambda_k1[j])) - jnp.exp(jnp.sum(lambda_q2[j] * lambda_k2[j]))
                   + lam_init).reshape(1, 1)
            sub_g = attn_subln[j].reshape(1, LANES)
            o_p = _flash_attention(qn[:tp].reshape(nb, sl, qkv_w), kv[:tp].reshape(nb, sl, 2 * qkv_w), lam, sub_g,
                                   nh=nh, hd=hd, out_scale=1.0 - lam_init)
            o_s = _decode_attention(qn[tp:].reshape(db, dl, qkv_w), kv[tp:].reshape(db, dl, 2 * qkv_w), cache4[j],
                                    page_table, lam, sub_g, nh=nh, hd=hd, out_scale=1.0 - lam_init)
            wts = _s5_weights(s5_lambda_re[j], s5_lambda_im[j], s5_b_re[j], s5_b_im[j], s5_c_re[j], s5_c_im[j],
                              s5_d[j], s5_log_dt[j], s5_w_glu[j], s5_b_glu[j])
            z_s5 = jnp.zeros((1, nb, ng * ns), F32)
            y_p, hr_p, hi_p = _s5(proj, z_s5, z_s5, wts, row0=0, nchunk=1, nseq=nb, slen=sl, ts=ts_p, prompt=True)
            y_s, hr_s, hi_s = _s5(proj, state_s5_re[j].reshape(db // rb, rb, ng * ns),
                                  state_s5_im[j].reshape(db // rb, rb, ng * ns), wts,
                                  row0=tp, nchunk=db // rb, nseq=rb, slen=dl, ts=dl, prompt=False)
            s5_re.append((hr_p.reshape(nb, ng, ns), hr_s.reshape(db, ng, ns)))
            s5_im.append((hi_p.reshape(nb, ng, ns), hi_s.reshape(db, ng, ns)))
            y_mix = jnp.concatenate([y_p.reshape(tp, s5_ch), y_s], axis=0)
            o_mix = jnp.concatenate([o_p.reshape(tp, qkv_w), o_s.reshape(tsmp, qkv_w)], axis=0)
            w_out = w_out_even[j].astype(BF16)
            x = _linear([y_mix, o_mix], [w_out[:s5_ch], w_out[s5_ch:]], resid=(x, g1), name="out_proj_even")
        else:
            proj = _linear([x], [w_in_odd[j].astype(BF16)], norm=nm, tn=dr, name="in_proj_odd").reshape(t, -1)
            sp = jax.nn.softplus(-lru_lambda[j]).reshape(1, dr)
            wts = (conv_w[j], conv_b[j].reshape(1, dr), lru_w_a[j].astype(BF16), lru_b_a[j].reshape(1, dr),
                   lru_w_x[j].astype(BF16), lru_b_x[j].reshape(1, dr), sp)
            pad = SUBLANES - (kw - 1)
            y_p, buf_p, h_p = _lru(proj, jnp.zeros((1, nb, SUBLANES, dr), F32), jnp.zeros((1, nb, dr), F32), wts,
                                   row0=0, nchunk=1, nseq=nb, slen=sl, ts=ts_p, prompt=True)
            buf_s0 = jnp.pad(state_conv[j], ((0, 0), (pad, 0), (0, 0))).reshape(db // rb, rb, SUBLANES, dr)
            y_s, buf_s, h_s = _lru(proj, buf_s0, state_lru[j].reshape(db // rb, rb, dr), wts,
                                   row0=tp, nchunk=db // rb, nseq=rb, slen=dl, ts=dl, prompt=False)
            lru_h.append((h_p.reshape(nb, dr), h_s.reshape(db, dr)))
            conv_bufs.append((buf_p.reshape(nb, SUBLANES, dr)[:, pad:], buf_s.reshape(db, SUBLANES, dr)[:, pad:]))
            y_mix = jnp.concatenate([y_p.reshape(tp, dr), y_s], axis=0)
            x = _linear([y_mix], [w_out_odd[j].astype(BF16)], resid=(x, g1), name="out_proj_odd")

        keys = peer_keys[layer]
        kexp = [jnp.einsum("hkd,hg->khgd", keys[:, s], eye_h).reshape(nkeys * p_heads, -1).astype(BF16)
                for s in range(2)]
        hf, s1, s2 = _peer_scores(x, (norm_ffn[layer].reshape(1, 1, dm), sc2, sh2), peer_wq[layer].astype(BF16),
                                  kexp[0], kexp[1], nh=p_heads)
        e1, e2, gates = _peer_topk(s1, s2, nkeys=nkeys, nh=p_heads)
        x = _peer_dense(hf, e1, e2, gates, peer_u[layer].astype(BF16), peer_v[layer].astype(BF16), x, g2,
                        nkeys=nkeys)

    x2 = x.reshape(t, dm)
    y_prompt = x2[:tp].reshape(nb, sl, dm)
    y_sample = x2[tp:].reshape(db, dl, dm)
    kv_all = jnp.stack(kv_rows)
    kv_prompt = kv_all[:, :tp].reshape(n_even, nb, sl, 2, nh, 2 * hd)
    kv_sample = kv_all[:, tp:].reshape(n_even, db, dl, 2, nh, 2 * hd)
    pick = lambda lst, i: jnp.stack([e[i] for e in lst])
    return (y_prompt, y_sample, kv_prompt, kv_sample, pick(s5_re, 0), pick(s5_im, 0), pick(s5_re, 1),
            pick(s5_im, 1), pick(lru_h, 0), pick(lru_h, 1), pick(conv_bufs, 0), pick(conv_bufs, 1))
```

```python
import functools
import math

import jax
import jax.numpy as jnp
from jax import lax
from jax.experimental import pallas as pl
from jax.experimental.pallas import tpu as pltpu

F32 = jnp.float32
BF16 = jnp.bfloat16
EPS = 1e-6
ROPE_THETA = 500000.0
LRU_C = 8.0
PEER_TOPK = 16
LANES = 128
SUBLANES = 8
NEG_INF = float("-inf")


def _params(sem, vmem_mb=48):
    return pltpu.CompilerParams(dimension_semantics=sem, vmem_limit_bytes=vmem_mb * 1024 * 1024)


def _gelu(x):
    return 0.5 * x * (1.0 + jnp.tanh(math.sqrt(2.0 / math.pi) * (x + 0.044715 * (x * x * x))))


def _sigmoid(x):
    return 1.0 / (1.0 + jnp.exp(-x))


def _mod_kernel(c_ref, w_ref, b_ref, o_ref):
    c = c_ref[...]
    a = (c * _sigmoid(c)).astype(BF16)
    o_ref[...] = jnp.dot(a, w_ref[...], preferred_element_type=F32) + b_ref[...]


def _mod(c_all, w, b):
    m, k = c_all.shape
    n = w.shape[1]
    tn = 1024
    return pl.pallas_call(
        _mod_kernel,
        grid=(n // tn,),
        in_specs=[pl.BlockSpec((m, k), lambda j: (0, 0)),
                  pl.BlockSpec((k, tn), lambda j: (0, j)),
                  pl.BlockSpec((1, tn), lambda j: (0, j))],
        out_specs=pl.BlockSpec((m, tn), lambda j: (0, j)),
        out_shape=jax.ShapeDtypeStruct((m, n), F32),
        compiler_params=_params(("arbitrary",)),
        name="adaln_mod",
    )(c_all, w, b.reshape(1, n))


def _norm_mod(x3, g_ref, sc_ref, sh_ref):
    ms = jnp.mean(x3 * x3, axis=-1, keepdims=True)
    y = x3 * lax.rsqrt(ms + EPS) * g_ref[...]
    return y * (1.0 + sc_ref[...]) + sh_ref[...]


def _linear_kernel(*refs, n_x, has_norm, has_resid, tm):
    it = iter(refs)
    x_refs = [next(it) for _ in range(n_x)]
    w_refs = [next(it) for _ in range(n_x)]
    if has_norm:
        g_ref, sc_ref, sh_ref = next(it), next(it), next(it)
    if has_resid:
        r_ref, gt_ref = next(it), next(it)
    o_ref = next(it)
    if has_norm:
        xn_ref = next(it)

        @pl.when(pl.program_id(1) == 0)
        def _():
            y = _norm_mod(x_refs[0][...], g_ref, sc_ref, sh_ref)
            xn_ref[...] = y.reshape(tm, y.shape[-1]).astype(BF16)

        acc = jnp.dot(xn_ref[...], w_refs[0][...], preferred_element_type=F32)
    else:
        acc = None
        for x_ref, w_ref in zip(x_refs, w_refs):
            d = jnp.dot(x_ref[...].astype(BF16), w_ref[...], preferred_element_type=F32)
            acc = d if acc is None else acc + d
    y3 = acc.reshape(tm // SUBLANES, SUBLANES, acc.shape[-1])
    if has_resid:
        y3 = r_ref[...] + gt_ref[...] * y3
    o_ref[...] = y3.astype(o_ref.dtype)


def _linear(xs, ws, *, norm=None, resid=None, out_dtype=F32, tm=512, tn=1024, name="linear"):
    has_norm, has_resid = norm is not None, resid is not None
    n = ws[0].shape[1]
    tn = min(tn, n)
    if has_norm:
        t8, _, k = xs[0].shape
        t = t8 * SUBLANES
    else:
        t = xs[0].shape[0]
    assert t % tm == 0 and n % tn == 0
    g8 = tm // SUBLANES
    in_specs, args = [], []
    for x in xs:
        if has_norm:
            in_specs.append(pl.BlockSpec((g8, SUBLANES, x.shape[-1]), lambda i, j: (i, 0, 0)))
        else:
            in_specs.append(pl.BlockSpec((tm, x.shape[-1]), lambda i, j: (i, 0)))
        args.append(x)
    for w in ws:
        in_specs.append(pl.BlockSpec((w.shape[0], tn), lambda i, j: (0, j)))
        args.append(w)
    scratch = []
    if has_norm:
        k = xs[0].shape[-1]
        in_specs += [pl.BlockSpec((1, 1, k), lambda i, j: (0, 0, 0)),
                     pl.BlockSpec((g8, 1, k), lambda i, j: (i, 0, 0)),
                     pl.BlockSpec((g8, 1, k), lambda i, j: (i, 0, 0))]
        args += list(norm)
        scratch.append(pltpu.VMEM((tm, k), BF16))
    if has_resid:
        in_specs += [pl.BlockSpec((g8, SUBLANES, tn), lambda i, j: (i, 0, j)),
                     pl.BlockSpec((g8, 1, tn), lambda i, j: (i, 0, j))]
        args += list(resid)
    return pl.pallas_call(
        functools.partial(_linear_kernel, n_x=len(xs), has_norm=has_norm, has_resid=has_resid, tm=tm),
        grid=(t // tm, n // tn),
        in_specs=in_specs,
        out_specs=pl.BlockSpec((g8, SUBLANES, tn), lambda i, j: (i, 0, j)),
        out_shape=jax.ShapeDtypeStruct((t // SUBLANES, SUBLANES, n), out_dtype),
        scratch_shapes=scratch,
        compiler_params=_params(("parallel", "arbitrary")),
        name=name,
    )(*args)


def _qkprep_kernel(q_ref, k_ref, v_ref, gq_ref, gk_ref, c_ref, s1_ref, s2_ref, p_ref, qo_ref, kv_ref, *,
                   nh, hd, q_scale):
    cos, sin_lo, sin_hi = c_ref[...], s1_ref[...], s2_ref[...]
    ones_blk = p_ref[...]
    rot = hd // 8

    def prep(x, g):
        sq = x * x
        hi = sq.astype(BF16)
        lo = (sq - hi.astype(F32)).astype(BF16)
        ss = (jnp.dot(hi, ones_blk, preferred_element_type=F32)
              + jnp.dot(lo, ones_blk, preferred_element_type=F32))
        y = x * lax.rsqrt(ss * (1.0 / hd) + EPS) * g
        return y * cos + pltpu.roll(y, LANES - rot, 1) * sin_lo + pltpu.roll(y, rot, 1) * sin_hi

    for h in range(nh):
        sl = slice(h * LANES, (h + 1) * LANES)
        qo_ref[:, sl] = (prep(q_ref[:, sl], gq_ref[...]) * q_scale).astype(qo_ref.dtype)
        kv_ref[:, sl] = prep(k_ref[:, sl], gk_ref[...])
    kv_ref[:, nh * LANES:] = v_ref[...]


def _qkprep(proj, gq, gk, cos_t, sin_lo_t, sin_hi_t, ones_blk, *, nh, hd, tm=512):
    t, n = proj.shape
    w = nh * 2 * hd
    assert 2 * hd == LANES and n == 4 * w and t % tm == 0
    row = lambda i: (i, 0)
    full = lambda i: (0, 0)
    return pl.pallas_call(
        functools.partial(_qkprep_kernel, nh=nh, hd=hd, q_scale=hd ** -0.5),
        grid=(t // tm,),
        in_specs=[pl.BlockSpec((tm, w), lambda i: (i, 1)),
                  pl.BlockSpec((tm, w), lambda i: (i, 2)),
                  pl.BlockSpec((tm, w), lambda i: (i, 3)),
                  pl.BlockSpec((1, LANES), full), pl.BlockSpec((1, LANES), full),
                  pl.BlockSpec((tm, LANES), row), pl.BlockSpec((tm, LANES), row), pl.BlockSpec((tm, LANES), row),
                  pl.BlockSpec((LANES, LANES), full)],
        out_specs=[pl.BlockSpec((tm, w), row), pl.BlockSpec((tm, 2 * w), row)],
        out_shape=[jax.ShapeDtypeStruct((t, w), BF16), jax.ShapeDtypeStruct((t, 2 * w), F32)],
        compiler_params=_params(("parallel",)),
        name="qk_prep",
    )(proj, proj, proj, gq, gk, cos_t, sin_lo_t, sin_hi_t, ones_blk)


def _subln(o, g_ref, out_scale):
    return o * lax.rsqrt(jnp.mean(o * o, axis=-1, keepdims=True) + EPS) * g_ref[...] * out_scale


def _online_softmax_step(s, v, m_ref, l_ref, acc_ref, idx):
    m_prev = m_ref[idx]
    m_new = jnp.maximum(m_prev, jnp.max(s, axis=-1, keepdims=True))
    alpha = jnp.exp(m_prev - m_new)
    p = jnp.exp(s - m_new)
    l_ref[idx] = alpha * l_ref[idx] + jnp.sum(p, axis=-1, keepdims=True)
    acc_ref[idx] = alpha * acc_ref[idx] + jnp.dot(p.astype(BF16), v, preferred_element_type=F32)
    m_ref[idx] = m_new


def _flash_kernel(lam_ref, q_ref, k_ref, v_ref, g_ref, o_ref, m_ref, l_ref, acc_ref, *, tq, hd, out_scale):
    qi, ki = pl.program_id(2), pl.program_id(3)

    @pl.when(ki == 0)
    def _():
        m_ref[...] = jnp.full(m_ref.shape, NEG_INF, F32)
        l_ref[...] = jnp.zeros(l_ref.shape, F32)
        acc_ref[...] = jnp.zeros(acc_ref.shape, F32)

    def step(masked):
        q = q_ref[0]
        k = k_ref[0].astype(BF16)
        v = v_ref[0].astype(BF16)
        lane = lax.broadcasted_iota(jnp.int32, q.shape, 1)
        for s in range(2):
            qs = jnp.where((lane >= s * hd) == (lane < (s + 1) * hd), q, jnp.zeros_like(q))
            sc = lax.dot_general(qs, k, (((1,), (1,)), ((), ())), preferred_element_type=F32)
            if masked:
                row = lax.broadcasted_iota(jnp.int32, sc.shape, 0)
                col = lax.broadcasted_iota(jnp.int32, sc.shape, 1)
                sc = jnp.where(col <= row, sc, NEG_INF)
            _online_softmax_step(sc, v, m_ref, l_ref, acc_ref, s)

    @pl.when(ki < qi)
    def _():
        step(False)

    @pl.when(ki == qi)
    def _():
        step(True)
        o = acc_ref[0] / l_ref[0] - lam_ref[...] * (acc_ref[1] / l_ref[1])
        o_ref[0] = _subln(o, g_ref, out_scale).astype(o_ref.dtype)


def _flash_attention(q3, kv3, lam, subln_g, *, nh, hd, out_scale, tq=1024):
    b, l, w = q3.shape
    tq = min(tq, l)
    assert l % tq == 0
    nq = l // tq
    return pl.pallas_call(
        functools.partial(_flash_kernel, tq=tq, hd=hd, out_scale=out_scale),
        grid=(b, nh, nq, nq),
        in_specs=[pl.BlockSpec((1, 1), lambda bb, h, qi, ki: (0, 0)),
                  pl.BlockSpec((1, tq, LANES), lambda bb, h, qi, ki: (bb, qi, h)),
                  pl.BlockSpec((1, tq, LANES), lambda bb, h, qi, ki: (bb, jnp.minimum(ki, qi), h)),
                  pl.BlockSpec((1, tq, LANES), lambda bb, h, qi, ki: (bb, jnp.minimum(ki, qi), nh + h)),
                  pl.BlockSpec((1, LANES), lambda bb, h, qi, ki: (0, 0))],
        out_specs=pl.BlockSpec((1, tq, LANES), lambda bb, h, qi, ki: (bb, qi, h)),
        out_shape=jax.ShapeDtypeStruct((b, l, w), BF16),
        scratch_shapes=[pltpu.VMEM((2, tq, 1), F32), pltpu.VMEM((2, tq, 1), F32), pltpu.VMEM((2, tq, LANES), F32)],
        compiler_params=_params(("parallel", "parallel", "parallel", "arbitrary")),
        name="diff_attn_prompt",
    )(lam, q3, kv3, kv3, subln_g)


def _decode_kernel(pt_ref, lam_ref, q_ref, kvn_ref, *rest, npg, nh, hd, dl, psz, out_scale):
    page_refs = rest[:npg]
    g_ref, o_ref, qw_s, s_s, v_s, m_s = rest[npg:]
    p = pl.program_id(1)
    cols = nh * 2 * dl
    nt = (((1,), (1,)), ((), ()))

    @pl.when(p == 0)
    def _():
        q = q_ref[0].astype(F32)
        lane = lax.broadcasted_iota(jnp.int32, (dl, LANES), 1)
        pieces = []
        for h in range(nh):
            qh = q[:, h * LANES:(h + 1) * LANES]
            for s in range(2):
                pieces.append(jnp.where((lane >= s * hd) == (lane < (s + 1) * hd), qh, 0.0))
        qw_s[...] = jnp.concatenate(pieces, axis=0).astype(BF16)
        m_s[...] = jnp.full(m_s.shape, NEG_INF, F32)

    for i, ref in enumerate(page_refs):
        pg = ref[...].reshape(psz, 2 * nh, LANES)
        k2 = pg[:, :nh, :].reshape(psz * nh, LANES).astype(BF16)
        v2 = pg[:, nh:, :].reshape(psz * nh, LANES).astype(BF16)
        sc = lax.dot_general(k2, qw_s[...], nt, preferred_element_type=F32)
        slot = p * npg + i
        s_s[slot] = sc
        v_s[slot] = v2
        m_s[...] = jnp.maximum(m_s[...], jnp.max(sc.reshape(psz, nh, cols), axis=0))

    @pl.when(p == pl.num_programs(1) - 1)
    def _():
        w = nh * LANES
        kvn = kvn_ref[0]
        pad = jnp.zeros((LANES - nh * dl, LANES), F32)
        kn = jnp.concatenate([kvn[:, h * LANES:(h + 1) * LANES] for h in range(nh)] + [pad], axis=0)
        vn = jnp.concatenate([kvn[:, w + h * LANES:w + (h + 1) * LANES] for h in range(nh)] + [pad], axis=0)
        sn = lax.dot_general(kn.astype(BF16), qw_s[...], nt, preferred_element_type=F32)
        r = lax.broadcasted_iota(jnp.int32, sn.shape, 0)
        c = lax.broadcasted_iota(jnp.int32, sn.shape, 1)
        same_head = r // dl == c // (2 * dl)
        causal = r % dl <= c % dl
        sn_m = jnp.where(same_head, jnp.where(causal, sn, NEG_INF), NEG_INF)
        hr = lax.broadcasted_iota(jnp.int32, (nh, cols), 0)
        hc = lax.broadcasted_iota(jnp.int32, (nh, cols), 1)
        diag = hr == hc // (2 * dl)
        m_col = jnp.maximum(jnp.max(sn_m, axis=0, keepdims=True),
                            jnp.max(jnp.where(diag, m_s[...], NEG_INF), axis=0, keepdims=True))
        pn = jnp.where(same_head, jnp.where(causal, jnp.exp(sn - m_col), 0.0), 0.0)
        l0 = jnp.sum(pn, axis=0, keepdims=True)
        o0 = jnp.dot(pn.T.astype(BF16), vn.astype(BF16), preferred_element_type=F32)

        def page_body(slot, carry):
            l_acc, o_acc = carry
            s3 = s_s[slot].reshape(psz, nh, cols)
            pm = jnp.where(diag[None], jnp.exp(s3 - m_col[None]), 0.0)
            l_acc = l_acc + jnp.sum(jnp.sum(pm, axis=0), axis=0, keepdims=True)
            pt = pm.reshape(psz * nh, cols).T.astype(BF16)
            return l_acc, o_acc + jnp.dot(pt, v_s[slot], preferred_element_type=F32)

        l_row, o = lax.fori_loop(0, s_s.shape[0], page_body, (l0, o0))
        er = lax.broadcasted_iota(jnp.int32, (cols, cols), 0)
        ec = lax.broadcasted_iota(jnp.int32, (cols, cols), 1)
        l_col = jnp.sum(jnp.where(er == ec, jnp.broadcast_to(l_row, (cols, cols)), 0.0), axis=1, keepdims=True)
        o = o / l_col
        for h in range(nh):
            r1 = slice(h * 2 * dl, h * 2 * dl + dl)
            r2 = slice(h * 2 * dl + dl, (h + 1) * 2 * dl)
            oo = o[r1] - lam_ref[...] * o[r2]
            o_ref[0, :, h * LANES:(h + 1) * LANES] = _subln(oo, g_ref, out_scale).astype(o_ref.dtype)


def _decode_attention(q3, kvn3, cache, layer, page_table, lam, subln_g, *, nh, hd, out_scale, npg=4):
    db, dl, w = q3.shape
    n_pages = page_table.shape[1]
    prow = cache.shape[2]
    psz = prow // (2 * nh)
    cols = nh * 2 * dl
    assert n_pages % npg == 0 and dl == SUBLANES and cols == LANES and 2 * hd == LANES

    def page_spec(i):
        return pl.BlockSpec((pl.Squeezed(), pl.Squeezed(), prow, LANES),
                            lambda b, p, pt: (layer, pt[b, p * npg + i], 0, 0))

    grid_spec = pltpu.PrefetchScalarGridSpec(
        num_scalar_prefetch=1,
        grid=(db, n_pages // npg),
        in_specs=[pl.BlockSpec((1, 1), lambda b, p, pt: (0, 0)),
                  pl.BlockSpec((1, dl, w), lambda b, p, pt: (b, 0, 0)),
                  pl.BlockSpec((1, dl, 2 * w), lambda b, p, pt: (b, 0, 0))]
        + [page_spec(i) for i in range(npg)]
        + [pl.BlockSpec((1, LANES), lambda b, p, pt: (0, 0))],
        out_specs=pl.BlockSpec((1, dl, w), lambda b, p, pt: (b, 0, 0)),
        scratch_shapes=[pltpu.VMEM((cols, LANES), BF16),
                        pltpu.VMEM((n_pages, psz * nh, cols), F32),
                        pltpu.VMEM((n_pages, psz * nh, LANES), BF16),
                        pltpu.VMEM((nh, cols), F32)],
    )
    return pl.pallas_call(
        functools.partial(_decode_kernel, npg=npg, nh=nh, hd=hd, dl=dl, psz=psz, out_scale=out_scale),
        grid_spec=grid_spec,
        out_shape=jax.ShapeDtypeStruct((db, dl, w), BF16),
        compiler_params=_params(("parallel", "arbitrary")),
        name="diff_attn_sample",
    )(page_table, lam, q3, kvn3, *([cache] * npg), subln_g)


def _s5_kernel(*refs, n_u, nseq, ts, lc, npk, out3d):
    u_refs = refs[:n_u]
    (wb_ref, wc_ref, d_ref, ar_ref, ai_ref, h0r_ref, h0i_ref, wg_ref, bg_ref,
     y_ref, hr_out, hi_out, xr_s, xi_s, hst_s) = refs[n_u:]
    s = pl.program_id(1)
    nlb = xr_s.shape[0]
    nst = nlb * LANES
    pb = nlb // npk
    pw = pb * LANES
    cw = d_ref.shape[1] // npk
    gb = lc // LANES

    @pl.when(s == 0)
    def _():
        hst_s[0] = h0r_ref[0]
        hst_s[1] = h0i_ref[0]

    u = jnp.concatenate([r[...] for r in u_refs], axis=0) if n_u > 1 else u_refs[0][...]
    ub = u.astype(BF16)
    for pk in range(npk):
        x = jnp.dot(ub[:, pk * cw:(pk + 1) * cw], wb_ref[pk], preferred_element_type=F32)
        for q in range(pb):
            xr_s[pk * pb + q] = x[:, q * LANES:(q + 1) * LANES]
            xi_s[pk * pb + q] = x[:, pw + q * LANES:pw + (q + 1) * LANES]

    for c in range(nlb // gb):
        blks = list(range(c * gb, (c + 1) * gb))
        ar = [jnp.broadcast_to(ar_ref[:, q * LANES:(q + 1) * LANES], (nseq, LANES)) for q in blks]
        ai = [jnp.broadcast_to(ai_ref[:, q * LANES:(q + 1) * LANES], (nseq, LANES)) for q in blks]

        def body(t, carry, blks=blks, ar=ar, ai=ai):
            idx = pl.ds(t, nseq, stride=ts)
            out = []
            for n, q in enumerate(blks):
                hr, hi = carry[2 * n], carry[2 * n + 1]
                nr = ar[n] * hr - ai[n] * hi + xr_s[q, idx, :]
                ni = ar[n] * hi + ai[n] * hr + xi_s[q, idx, :]
                xr_s[q, idx, :] = nr
                xi_s[q, idx, :] = ni
                out += [nr, ni]
            return tuple(out)

        init = []
        for q in blks:
            init += [hst_s[0, :, q * LANES:(q + 1) * LANES], hst_s[1, :, q * LANES:(q + 1) * LANES]]
        fin = lax.fori_loop(0, ts, body, tuple(init))
        for n, q in enumerate(blks):
            hst_s[0, :, q * LANES:(q + 1) * LANES] = fin[2 * n]
            hst_s[1, :, q * LANES:(q + 1) * LANES] = fin[2 * n + 1]

    ys = []
    for pk in range(npk):
        hr_b = jnp.concatenate([xr_s[pk * pb + q] for q in range(pb)], axis=1).astype(BF16)
        hi_b = jnp.concatenate([xi_s[pk * pb + q] for q in range(pb)], axis=1).astype(BF16)
        ys.append(jnp.dot(hr_b, wc_ref[0, pk], preferred_element_type=F32)
                  + jnp.dot(hi_b, wc_ref[1, pk], preferred_element_type=F32))
    y = jnp.concatenate(ys, axis=1) + d_ref[...] * u
    g = _gelu(y)
    out = g * _sigmoid(jnp.dot(g.astype(BF16), wg_ref[...], preferred_element_type=F32) + bg_ref[...])
    if out3d:
        y_ref[...] = out.reshape(y_ref.shape).astype(y_ref.dtype)
    else:
        y_ref[...] = out.astype(y_ref.dtype)

    @pl.when(s == pl.num_programs(1) - 1)
    def _():
        hr_out[0] = hst_s[0]
        hi_out[0] = hst_s[1]


def _s5(proj, h0r, h0i, wts, *, row0, nchunk, nseq, slen, ts, prompt):
    wb, wc, dd, ar, ai, wg, bg = wts
    npk = wb.shape[0]
    ch = dd.shape[1]
    nst = ar.shape[1]
    nsteps = slen // ts
    rows = nseq * ts
    lc = max(LANES, 512 // (-(-nseq // SUBLANES)))
    if prompt:
        assert nchunk == 1 and row0 == 0
        u_specs = [pl.BlockSpec((ts, ch), lambda c, s, q=q: (q * nsteps + s, 0)) for q in range(nseq)]
        y_spec = pl.BlockSpec((nseq, ts, ch), lambda c, s: (0, s, 0))
        y_shape = jax.ShapeDtypeStruct((nseq, slen, ch), BF16)
    else:
        assert nsteps == 1 and row0 % rows == 0
        u_specs = [pl.BlockSpec((rows, ch), lambda c, s: (row0 // rows + c, 0))]
        y_spec = pl.BlockSpec((rows, ch), lambda c, s: (c, 0))
        y_shape = jax.ShapeDtypeStruct((nchunk * rows, ch), BF16)
    n_u = len(u_specs)
    full2 = lambda c, s: (0, 0)
    st_spec = pl.BlockSpec((1, nseq, nst), lambda c, s: (c, 0, 0))
    return pl.pallas_call(
        functools.partial(_s5_kernel, n_u=n_u, nseq=nseq, ts=ts, lc=lc, npk=npk, out3d=prompt),
        grid=(nchunk, nsteps),
        in_specs=u_specs + [
            pl.BlockSpec(wb.shape, lambda c, s: (0, 0, 0)),
            pl.BlockSpec(wc.shape, lambda c, s: (0, 0, 0, 0)),
            pl.BlockSpec((1, ch), full2), pl.BlockSpec((1, nst), full2), pl.BlockSpec((1, nst), full2),
            st_spec, st_spec,
            pl.BlockSpec(wg.shape, full2), pl.BlockSpec((1, ch), full2)],
        out_specs=[y_spec, st_spec, st_spec],
        out_shape=[y_shape, jax.ShapeDtypeStruct((nchunk, nseq, nst), F32),
                   jax.ShapeDtypeStruct((nchunk, nseq, nst), F32)],
        scratch_shapes=[pltpu.VMEM((nst // LANES, rows, LANES), F32), pltpu.VMEM((nst // LANES, rows, LANES), F32),
                        pltpu.VMEM((2, nseq, nst), F32)],
        compiler_params=_params(("parallel", "arbitrary")),
        name="s5_prompt" if prompt else "s5_sample",
    )(*([proj] * n_u), wb, wc, dd, ar, ai, h0r, h0i, wg, bg)


def _lru_kernel(*refs, n_x, nseq, ts, lc, nblk, out3d):
    g_refs = refs[:n_x]
    x_refs = refs[n_x:2 * n_x]
    (cw_ref, cb_ref, wa_ref, ba_ref, wx_ref, bx_ref, sp_ref, buf_ref, h0_ref,
     y_ref, buf_out, h_out, a_s, b_s, prev_s, h_s) = refs[2 * n_x:]
    s = pl.program_id(1)
    dr = cb_ref.shape[1]
    bs = dr // nblk
    rows = nseq * ts

    @pl.when(s == 0)
    def _():
        prev_s[...] = buf_ref[0]
        h_s[...] = h0_ref[0]

    if n_x > 1:
        x3 = jnp.stack([r[...] for r in x_refs], axis=0)
        gate = jnp.concatenate([r[...] for r in g_refs], axis=0)
    else:
        x3 = x_refs[0][...].reshape(nseq, ts, dr)
        gate = g_refs[0][...]
    xx = jnp.concatenate([prev_s[...], x3], axis=1)
    kw = cw_ref.shape[0]
    conv = cb_ref[...] + xx[:, SUBLANES - (kw - 1):SUBLANES - (kw - 1) + ts] * cw_ref[0:1, :]
    for tap in range(1, kw):
        off = SUBLANES - (kw - 1) + tap
        conv = conv + xx[:, off:off + ts] * cw_ref[tap:tap + 1, :]
    prev_s[...] = xx[:, ts:ts + SUBLANES]

    c2 = conv.reshape(rows, dr)
    cb16 = c2.astype(BF16)
    rs, is_ = [], []
    for n in range(nblk):
        blk = cb16[:, n * bs:(n + 1) * bs]
        rs.append(jnp.dot(blk, wa_ref[n], preferred_element_type=F32))
        is_.append(jnp.dot(blk, wx_ref[n], preferred_element_type=F32))
    r = _sigmoid(jnp.concatenate(rs, axis=1) + ba_ref[...])
    i = _sigmoid(jnp.concatenate(is_, axis=1) + bx_ref[...])
    log_a = -LRU_C * r * sp_ref[...]
    a_all = jnp.exp(log_a)
    b_all = jnp.sqrt(1.0 - jnp.exp(2.0 * log_a)) * (i * c2)
    nlb = dr // LANES
    for q in range(nlb):
        a_s[q] = a_all[:, q * LANES:(q + 1) * LANES]
        b_s[q] = b_all[:, q * LANES:(q + 1) * LANES]

    gb = lc // LANES
    for c in range(nlb // gb):
        blks = list(range(c * gb, (c + 1) * gb))

        def body(t, carry, blks=blks):
            idx = pl.ds(t, nseq, stride=ts)
            out = []
            for n, q in enumerate(blks):
                h = a_s[q, idx, :] * carry[n] + b_s[q, idx, :]
                b_s[q, idx, :] = h
                out.append(h)
            return tuple(out)

        fin = lax.fori_loop(0, ts, body, tuple(h_s[:, q * LANES:(q + 1) * LANES] for q in blks))
        for n, q in enumerate(blks):
            h_s[:, q * LANES:(q + 1) * LANES] = fin[n]

    out = _gelu(gate) * jnp.concatenate([b_s[q] for q in range(nlb)], axis=1)
    if out3d:
        y_ref[...] = out.reshape(y_ref.shape).astype(y_ref.dtype)
    else:
        y_ref[...] = out.astype(y_ref.dtype)

    @pl.when(s == pl.num_programs(1) - 1)
    def _():
        buf_out[0] = prev_s[...]
        h_out[0] = h_s[...]


def _lru(proj, buf8, h0, wts, *, row0, nchunk, nseq, slen, ts, prompt):
    cw, cb, wa, ba, wx, bx, sp = wts
    dr = cb.shape[1]
    nblk = wa.shape[0]
    nsteps = slen // ts
    rows = nseq * ts
    lc = dr if nseq <= SUBLANES else 256
    assert dr % lc == 0
    if prompt:
        assert nchunk == 1 and row0 == 0
        g_specs = [pl.BlockSpec((ts, dr), lambda c, s, q=q: (q * nsteps + s, 0)) for q in range(nseq)]
        x_specs = [pl.BlockSpec((ts, dr), lambda c, s, q=q: (q * nsteps + s, 1)) for q in range(nseq)]
        y_spec = pl.BlockSpec((nseq, ts, dr), lambda c, s: (0, s, 0))
        y_shape = jax.ShapeDtypeStruct((nseq, slen, dr), BF16)
    else:
        assert nsteps == 1 and row0 % rows == 0 and ts == SUBLANES
        g_specs = [pl.BlockSpec((rows, dr), lambda c, s: (row0 // rows + c, 0))]
        x_specs = [pl.BlockSpec((rows, dr), lambda c, s: (row0 // rows + c, 1))]
        y_spec = pl.BlockSpec((rows, dr), lambda c, s: (c, 0))
        y_shape = jax.ShapeDtypeStruct((nchunk * rows, dr), BF16)
    n_x = len(x_specs)
    full2 = lambda c, s: (0, 0)
    full3 = lambda c, s: (0, 0, 0)
    return pl.pallas_call(
        functools.partial(_lru_kernel, n_x=n_x, nseq=nseq, ts=ts, lc=lc, nblk=nblk, out3d=prompt),
        grid=(nchunk, nsteps),
        in_specs=g_specs + x_specs + [
            pl.BlockSpec(cw.shape, full2), pl.BlockSpec((1, dr), full2),
            pl.BlockSpec(wa.shape, full3), pl.BlockSpec((1, dr), full2),
            pl.BlockSpec(wx.shape, full3), pl.BlockSpec((1, dr), full2),
            pl.BlockSpec((1, dr), full2),
            pl.BlockSpec((1, nseq, SUBLANES, dr), lambda c, s: (c, 0, 0, 0)),
            pl.BlockSpec((1, nseq, dr), lambda c, s: (c, 0, 0))],
        out_specs=[y_spec,
                   pl.BlockSpec((1, nseq, SUBLANES, dr), lambda c, s: (c, 0, 0, 0)),
                   pl.BlockSpec((1, nseq, dr), lambda c, s: (c, 0, 0))],
        out_shape=[y_shape, jax.ShapeDtypeStruct((nchunk, nseq, SUBLANES, dr), F32),
                   jax.ShapeDtypeStruct((nchunk, nseq, dr), F32)],
        scratch_shapes=[pltpu.VMEM((dr // LANES, rows, LANES), F32), pltpu.VMEM((dr // LANES, rows, LANES), F32),
                        pltpu.VMEM((nseq, SUBLANES, dr), F32), pltpu.VMEM((nseq, dr), F32)],
        compiler_params=_params(("parallel", "arbitrary")),
        name="rglru_prompt" if prompt else "rglru_sample",
    )(*([proj] * (2 * n_x)), cw, cb, wa, ba, wx, bx, sp, buf8, h0)


def _peer_scores_kernel(x_ref, g_ref, sc_ref, sh_ref, wq_ref, k1_ref, k2_ref, hf_ref, s1_ref, s2_ref, *,
                        tm, nh, dk):
    y = _norm_mod(x_ref[...], g_ref, sc_ref, sh_ref)
    hf = y.reshape(tm, y.shape[-1]).astype(BF16)
    hf_ref[...] = hf
    q = jnp.dot(hf, wq_ref[...], preferred_element_type=F32)
    half = dk // 2
    q1, q2 = [], []
    for h in range(nh):
        qh = q[:, h * dk:(h + 1) * dk]
        qn = qh * lax.rsqrt(jnp.mean(qh * qh, axis=-1, keepdims=True) + EPS)
        q1.append(qn[:, :half])
        q2.append(qn[:, half:])
    nt = (((1,), (1,)), ((), ()))
    s1_ref[...] = lax.dot_general(k1_ref[...], jnp.concatenate(q1, axis=1).astype(BF16), nt,
                                  preferred_element_type=F32)
    s2_ref[...] = lax.dot_general(k2_ref[...], jnp.concatenate(q2, axis=1).astype(BF16), nt,
                                  preferred_element_type=F32)


def _peer_scores(x3, norm, wq, kexp1, kexp2, *, nh, tm=512):
    t8, _, d = x3.shape
    t = t8 * SUBLANES
    assert t % tm == 0
    g8 = tm // SUBLANES
    nq = wq.shape[1]
    nrow = kexp1.shape[0]
    full2 = lambda i: (0, 0)
    return pl.pallas_call(
        functools.partial(_peer_scores_kernel, tm=tm, nh=nh, dk=nq // nh),
        grid=(t // tm,),
        in_specs=[pl.BlockSpec((g8, SUBLANES, d), lambda i: (i, 0, 0)),
                  pl.BlockSpec((1, 1, d), lambda i: (0, 0, 0)),
                  pl.BlockSpec((g8, 1, d), lambda i: (i, 0, 0)),
                  pl.BlockSpec((g8, 1, d), lambda i: (i, 0, 0)),
                  pl.BlockSpec(wq.shape, full2), pl.BlockSpec(kexp1.shape, full2), pl.BlockSpec(kexp2.shape, full2)],
        out_specs=[pl.BlockSpec((tm, d), lambda i: (i, 0)),
                   pl.BlockSpec((nrow, tm), lambda i: (0, i)),
                   pl.BlockSpec((nrow, tm), lambda i: (0, i))],
        out_shape=[jax.ShapeDtypeStruct((t, d), BF16), jax.ShapeDtypeStruct((nrow, t), F32),
                   jax.ShapeDtypeStruct((nrow, t), F32)],
        compiler_params=_params(("parallel",)),
        name="peer_scores",
    )(x3, *norm, wq, kexp1, kexp2)


def _hyperbola(k):
    return [(a, b) for a in range(k) for b in range(k) if (a + 1) * (b + 1) <= k]


def _peer_topk_kernel(s1_ref, s2_ref, e1_ref, e2_ref, g_ref, x_s, v_s, i_s, *, nkeys, nh, topk):
    tl = s1_ref.shape[1]
    kid = lax.broadcasted_iota(jnp.int32, (nkeys, nh, tl), 0)

    for side, s_ref in enumerate((s1_ref, s2_ref)):
        x_s[...] = s_ref[...].reshape(nkeys, nh, tl)

        def body(it, carry, side=side):
            x = x_s[...]
            m = jnp.max(x, axis=0)
            idx = jnp.min(jnp.where(x == m[None], kid, nkeys), axis=0)
            v_s[side, it] = m
            i_s[side, it] = idx
            x_s[...] = jnp.where(kid == idx[None], NEG_INF, x)
            return carry

        lax.fori_loop(0, topk, body, 0)

    pairs = _hyperbola(topk)
    v1 = [v_s[0, k] for k in range(topk)]
    v2 = [v_s[1, k] for k in range(topk)]
    cand = [v1[a] + v2[b] for a, b in pairs]
    flat = [a * topk + b for a, b in pairs]
    big = topk * topk
    shift = topk.bit_length() - 1
    assert 1 << shift == topk
    tops, k1s, k2s = [], [], []
    for _ in range(topk):
        m = functools.reduce(jnp.maximum, cand)
        sel = functools.reduce(jnp.minimum, [jnp.where(c == m, f, big) for c, f in zip(cand, flat)])
        cand = [jnp.where(sel == f, NEG_INF, c) for c, f in zip(cand, flat)]
        tops.append(m)
        k1s.append(lax.shift_right_logical(sel, shift))
        k2s.append(sel & (topk - 1))
    i1 = [i_s[0, k] for k in range(topk)]
    i2 = [i_s[1, k] for k in range(topk)]
    ex = [jnp.exp(tv - tops[0]) for tv in tops]
    den = functools.reduce(jnp.add, ex)
    e1, e2, gs = [], [], []
    for k in range(topk):
        e1.append(functools.reduce(jnp.add, [jnp.where(k1s[k] == j, i1[j], 0) for j in range(topk)]))
        e2.append(functools.reduce(jnp.add, [jnp.where(k2s[k] == j, i2[j], 0) for j in range(topk)]))
        gs.append(ex[k] / den)
    e1_ref[...] = jnp.concatenate(e1, axis=0).astype(F32).T
    e2_ref[...] = jnp.concatenate(e2, axis=0).astype(F32).T
    g_ref[...] = jnp.concatenate(gs, axis=0).T


def _peer_topk(s1, s2, *, nkeys, nh, topk=PEER_TOPK, tl=LANES):
    nrow, t = s1.shape
    assert nrow == nkeys * nh and t % tl == 0 and nh * topk == LANES
    in_spec = pl.BlockSpec((nrow, tl), lambda i: (0, i))
    out_spec = pl.BlockSpec((tl, nh * topk), lambda i: (i, 0))
    out = jax.ShapeDtypeStruct((t, nh * topk), F32)
    return pl.pallas_call(
        functools.partial(_peer_topk_kernel, nkeys=nkeys, nh=nh, topk=topk),
        grid=(t // tl,),
        in_specs=[in_spec, in_spec],
        out_specs=[out_spec, out_spec, out_spec],
        out_shape=[out, out, out],
        scratch_shapes=[pltpu.VMEM((nkeys, nh, tl), F32), pltpu.VMEM((2, topk, nh, tl), F32),
                        pltpu.VMEM((2, topk, nh, tl), jnp.int32)],
        compiler_params=_params(("parallel",)),
        name="peer_topk",
    )(s1, s2)


def _peer_dense_kernel(hf_ref, e1_ref, e2_ref, g_ref, u_ref, v_ref, r_ref, gt_ref, o_ref, g3_s, acc_s, *,
                       tb, nkeys, eb1):
    j = pl.program_id(1)

    @pl.when(j == 0)
    def _():
        acc_s[...] = jnp.zeros(acc_s.shape, F32)
        kio = lax.broadcasted_iota(jnp.int32, (nkeys, LANES), 0).astype(F32)
        nt = (((1,), (1,)), ((), ()))

        def build(t, carry):
            e1 = jnp.broadcast_to(e1_ref[pl.ds(t, 1), :], (nkeys, LANES))
            e2 = jnp.broadcast_to(e2_ref[pl.ds(t, 1), :], (nkeys, LANES))
            gg = jnp.broadcast_to(g_ref[pl.ds(t, 1), :], (nkeys, LANES))
            a = jnp.where(kio == e1, gg, 0.0)
            a_hi = a.astype(BF16)
            a_lo = (a - a_hi.astype(F32)).astype(BF16)
            b = jnp.where(kio == e2, 1.0, 0.0).astype(BF16)
            gmat = lax.dot_general(jnp.concatenate([a_hi, a_lo], axis=1), jnp.concatenate([b, b], axis=1), nt,
                                   preferred_element_type=F32)
            g3_s[pl.ds(pl.multiple_of(t * nkeys, nkeys), nkeys), :] = gmat
            return carry

        lax.fori_loop(0, tb, build, 0, unroll=4)

    z = lax.dot_general(hf_ref[...], u_ref[...], (((1,), (1,)), ((), ())), preferred_element_type=F32)
    act = _gelu(z)
    gate = jnp.concatenate(
        [g3_s[pl.ds(j * eb1 + a, tb, stride=nkeys), :] for a in range(eb1)], axis=1)
    acc_s[...] += jnp.dot((gate * act).astype(BF16), v_ref[...], preferred_element_type=F32)

    @pl.when(j == pl.num_programs(1) - 1)
    def _():
        o_ref[...] = r_ref[...] + gt_ref[...] * acc_s[...].reshape(o_ref.shape)


def _peer_dense(hf, e1, e2, g, u_tab, v_tab, resid, gate, *, nkeys, tb=256, eb1=8):
    t, d = hf.shape
    ne = u_tab.shape[0]
    assert ne == nkeys * nkeys and t % tb == 0 and nkeys % eb1 == 0
    eb = eb1 * nkeys
    g8 = tb // SUBLANES
    return pl.pallas_call(
        functools.partial(_peer_dense_kernel, tb=tb, nkeys=nkeys, eb1=eb1),
        grid=(t // tb, ne // eb),
        in_specs=[pl.BlockSpec((tb, d), lambda i, j: (i, 0)),
                  pl.BlockSpec((tb, LANES), lambda i, j: (i, 0)),
                  pl.BlockSpec((tb, LANES), lambda i, j: (i, 0)),
                  pl.BlockSpec((tb, LANES), lambda i, j: (i, 0)),
                  pl.BlockSpec((eb, d), lambda i, j: (j, 0)),
                  pl.BlockSpec((eb, d), lambda i, j: (j, 0)),
                  pl.BlockSpec((g8, SUBLANES, d), lambda i, j: (i, 0, 0)),
                  pl.BlockSpec((g8, 1, d), lambda i, j: (i, 0, 0))],
        out_specs=pl.BlockSpec((g8, SUBLANES, d), lambda i, j: (i, 0, 0)),
        out_shape=jax.ShapeDtypeStruct((t // SUBLANES, SUBLANES, d), F32),
        scratch_shapes=[pltpu.VMEM((tb * nkeys, LANES), F32), pltpu.VMEM((tb, d), F32)],
        compiler_params=_params(("parallel", "arbitrary"), vmem_mb=56),
        name="peer_dense",
    )(hf, e1, e2, g, u_tab, v_tab, resid, gate)


def _s5_weights(lam_re, lam_im, b_re, b_im, c_re, c_im, d, log_dt, w_glu, b_glu, *, gpp=8):
    ng, ns = lam_re.shape
    p = d.shape[1]
    dt = jnp.exp(log_dt)[:, None]
    mag = jnp.exp(lam_re * dt)
    ab_re, ab_im = mag * jnp.cos(lam_im * dt), mag * jnp.sin(lam_im * dt)
    den = lam_re * lam_re + lam_im * lam_im
    fr = ((ab_re - 1.0) * lam_re + ab_im * lam_im) / den
    fi = (ab_im * lam_re - (ab_re - 1.0) * lam_im) / den
    xb_re = fr[..., None] * b_re - fi[..., None] * b_im
    xb_im = fr[..., None] * b_im + fi[..., None] * b_re
    npk = ng // gpp
    eye = jnp.eye(gpp, dtype=F32)

    def pack_b(w):
        w = w.reshape(npk, gpp, ns, p)
        return jnp.einsum("kgnp,gh->kgphn", w, eye).reshape(npk, gpp * p, gpp * ns)

    def pack_c(w):
        w = w.reshape(npk, gpp, p, ns)
        return jnp.einsum("kgpn,gh->kgnhp", w, eye).reshape(npk, gpp * ns, gpp * p)

    wb = jnp.concatenate([pack_b(xb_re), pack_b(xb_im)], axis=2).astype(BF16)
    wc = jnp.stack([pack_c(c_re), -pack_c(c_im)], axis=0).astype(BF16)
    return (wb, wc, d.reshape(1, ng * p), ab_re.reshape(1, ng * ns), ab_im.reshape(1, ng * ns),
            w_glu.astype(BF16), b_glu.reshape(1, -1))


def _rope_tables(pos, hd):
    rot = hd // 4
    inv = jnp.power(ROPE_THETA, -jnp.arange(0, rot, 2, dtype=F32) / rot)
    ang = pos.astype(F32)[:, None] * inv[None, :]
    cos, sin = jnp.cos(ang), jnp.sin(ang)
    n = pos.shape[0]
    half = rot // 2
    ones = jnp.ones((n, hd - rot), F32)
    zeros = jnp.zeros((n, hd - rot), F32)
    zh = jnp.zeros((n, half), F32)
    cos_t = jnp.concatenate([cos, cos, ones], axis=1)
    sin_lo = jnp.concatenate([-sin, zh, zeros], axis=1)
    sin_hi = jnp.concatenate([zh, sin, zeros], axis=1)
    rep = LANES // hd
    return jnp.tile(cos_t, (1, rep)), jnp.tile(sin_lo, (1, rep)), jnp.tile(sin_hi, (1, rep))


def kernel(x_prompt, x_sample, cache_kv, page_table, state_s5_re, state_s5_im, state_lru, state_conv, c_prompt, c_sample, norm_mix, norm_ffn, w_ada, b_ada, w_in_even, w_out_even, q_norm, k_norm, lambda_q1, lambda_k1, lambda_q2, lambda_k2, attn_subln, s5_lambda_re, s5_lambda_im, s5_b_re, s5_b_im, s5_c_re, s5_c_im, s5_d, s5_log_dt, s5_w_glu, s5_b_glu, w_in_odd, conv_w, conv_b, lru_w_a, lru_b_a, lru_w_x, lru_b_x, lru_lambda, w_out_odd, peer_wq, peer_keys, peer_u, peer_v):
    nb, sl, dm = x_prompt.shape
    db, dl, _ = x_sample.shape
    depth = norm_mix.shape[0]
    n_even = w_in_even.shape[0]
    nh = cache_kv.shape[4]
    hd = cache_kv.shape[5] // 2
    psz = cache_kv.shape[2]
    n_pool = cache_kv.shape[1]
    qkv_w = nh * 2 * hd
    s5_ch = w_in_even.shape[2] - 3 * qkv_w
    ng, ns = s5_lambda_re.shape[1], s5_lambda_re.shape[2]
    dr = conv_b.shape[1]
    kw = conv_w.shape[1]
    p_heads, _, nkeys, _ = peer_keys.shape[1:]
    tp, tsmp = nb * sl, db * dl
    t = tp + tsmp
    t8 = t // SUBLANES
    past_len = page_table.shape[1] * psz

    x = jnp.concatenate([x_prompt.reshape(tp, dm), x_sample.reshape(tsmp, dm)], axis=0).reshape(t8, SUBLANES, dm)
    c_all = jnp.concatenate([c_prompt, c_sample], axis=0)

    def expand(m):
        return jnp.concatenate([jnp.repeat(m[:nb], sl // SUBLANES, axis=0),
                                jnp.repeat(m[nb:], dl // SUBLANES, axis=0)], axis=0)[:, None, :]

    pos = jnp.concatenate([jnp.tile(jnp.arange(sl, dtype=jnp.int32), nb),
                           jnp.tile(past_len + jnp.arange(dl, dtype=jnp.int32), db)])
    cos_t, sin_lo, sin_hi = _rope_tables(pos, hd)
    lane = jnp.arange(LANES)
    ones_blk = (lane[:, None] // hd == lane[None, :] // hd).astype(BF16)
    eye_h = jnp.eye(p_heads, dtype=F32)
    cache_rows = cache_kv.reshape(n_even, n_pool, psz * 2 * nh, 2 * hd)
    rb = 32 if db % 32 == 0 else db
    ts_p = 64 if sl % 64 == 0 else sl

    kv_rows, s5_re, s5_im, lru_h, conv_bufs = [], [], [], [], []
    for layer in range(depth):
        j = layer // 2
        mod = _mod(c_all, w_ada[layer].astype(BF16), b_ada[layer])
        sh1, sc1, g1, sh2, sc2, g2 = [expand(m) for m in jnp.split(mod, 6, axis=-1)]
        nm = (norm_mix[layer].reshape(1, 1, dm), sc1, sh1)
        if layer % 2 == 0:
            proj = _linear([x], [w_in_even[j].astype(BF16)], norm=nm, name="in_proj_even").reshape(t, -1)
            tile2 = lambda g: jnp.tile(g, LANES // hd).reshape(1, LANES)
            qn, kv = _qkprep(proj, tile2(q_norm[j]), tile2(k_norm[j]), cos_t, sin_lo, sin_hi, ones_blk, nh=nh, hd=hd)
            kv_rows.append(kv)
            lam_init = 0.8 - 0.6 * math.exp(-0.3 * layer)
            lam = (jnp.exp(jnp.sum(lambda_q1[j] * lambda_k1[j])) - jnp.exp(jnp.sum(lambda_q2[j] * lambda_k2[j]))
                   + lam_init).reshape(1, 1)
            sub_g = attn_subln[j].reshape(1, LANES)
            o_p = _flash_attention(qn[:tp].reshape(nb, sl, qkv_w), kv[:tp].reshape(nb, sl, 2 * qkv_w), lam, sub_g,
                                   nh=nh, hd=hd, out_scale=1.0 - lam_init)
            o_s = _decode_attention(qn[tp:].reshape(db, dl, qkv_w), kv[tp:].reshape(db, dl, 2 * qkv_w), cache_rows,
                                    j, page_table, lam, sub_g, nh=nh, hd=hd, out_scale=1.0 - lam_init)
            wts = _s5_weights(s5_lambda_re[j], s5_lambda_im[j], s5_b_re[j], s5_b_im[j], s5_c_re[j], s5_c_im[j],
                              s5_d[j], s5_log_dt[j], s5_w_glu[j], s5_b_glu[j])
            z_s5 = jnp.zeros((1, nb, ng * ns), F32)
            y_p, hr_p, hi_p = _s5(proj, z_s5, z_s5, wts, row0=0, nchunk=1, nseq=nb, slen=sl, ts=ts_p, prompt=True)
            y_s, hr_s, hi_s = _s5(proj, state_s5_re[j].reshape(db // rb, rb, ng * ns),
                                  state_s5_im[j].reshape(db // rb, rb, ng * ns), wts,
                                  row0=tp, nchunk=db // rb, nseq=rb, slen=dl, ts=dl, prompt=False)
            s5_re.append((hr_p.reshape(nb, ng, ns), hr_s.reshape(db, ng, ns)))
            s5_im.append((hi_p.reshape(nb, ng, ns), hi_s.reshape(db, ng, ns)))
            y_mix = jnp.concatenate([y_p.reshape(tp, s5_ch), y_s], axis=0)
            o_mix = jnp.concatenate([o_p.reshape(tp, qkv_w), o_s.reshape(tsmp, qkv_w)], axis=0)
            w_out = w_out_even[j].astype(BF16)
            x = _linear([y_mix, o_mix], [w_out[:s5_ch], w_out[s5_ch:]], resid=(x, g1), name="out_proj_even")
        else:
            proj = _linear([x], [w_in_odd[j].astype(BF16)], norm=nm, tn=dr, name="in_proj_odd").reshape(t, -1)
            sp = jax.nn.softplus(-lru_lambda[j]).reshape(1, dr)
            wts = (conv_w[j], conv_b[j].reshape(1, dr), lru_w_a[j].astype(BF16), lru_b_a[j].reshape(1, dr),
                   lru_w_x[j].astype(BF16), lru_b_x[j].reshape(1, dr), sp)
            pad = SUBLANES - (kw - 1)
            y_p, buf_p, h_p = _lru(proj, jnp.zeros((1, nb, SUBLANES, dr), F32), jnp.zeros((1, nb, dr), F32), wts,
                                   row0=0, nchunk=1, nseq=nb, slen=sl, ts=ts_p, prompt=True)
            buf_s0 = jnp.pad(state_conv[j], ((0, 0), (pad, 0), (0, 0))).reshape(db // rb, rb, SUBLANES, dr)
            y_s, buf_s, h_s = _lru(proj, buf_s0, state_lru[j].reshape(db // rb, rb, dr), wts,
                                   row0=tp, nchunk=db // rb, nseq=rb, slen=dl, ts=dl, prompt=False)
            lru_h.append((h_p.reshape(nb, dr), h_s.reshape(db, dr)))
            conv_bufs.append((buf_p.reshape(nb, SUBLANES, dr)[:, pad:], buf_s.reshape(db, SUBLANES, dr)[:, pad:]))
            y_mix = jnp.concatenate([y_p.reshape(tp, dr), y_s], axis=0)
            x = _linear([y_mix], [w_out_odd[j].astype(BF16)], resid=(x, g1), name="out_proj_odd")

        keys = peer_keys[layer]
        kexp = [jnp.einsum("hkd,hg->khgd", keys[:, s], eye_h).reshape(nkeys * p_heads, -1).astype(BF16)
                for s in range(2)]
        hf, s1, s2 = _peer_scores(x, (norm_ffn[layer].reshape(1, 1, dm), sc2, sh2), peer_wq[layer].astype(BF16),
                                  kexp[0], kexp[1], nh=p_heads)
        e1, e2, gates = _peer_topk(s1, s2, nkeys=nkeys, nh=p_heads)
        x = _peer_dense(hf, e1, e2, gates, peer_u[layer].astype(BF16), peer_v[layer].astype(BF16), x, g2,
                        nkeys=nkeys)

    x2 = x.reshape(t, dm)
    y_prompt = x2[:tp].reshape(nb, sl, dm)
    y_sample = x2[tp:].reshape(db, dl, dm)
    kv_all = jnp.stack(kv_rows)
    kv_prompt = kv_all[:, :tp].reshape(n_even, nb, sl, 2, nh, 2 * hd)
    kv_sample = kv_all[:, tp:].reshape(n_even, db, dl, 2, nh, 2 * hd)
    pick = lambda lst, i: jnp.stack([e[i] for e in lst])
    return (y_prompt, y_sample, kv_prompt, kv_sample, pick(s5_re, 0), pick(s5_im, 0), pick(s5_re, 1),
            pick(s5_im, 1), pick(lru_h, 0), pick(lru_h, 1), pick(conv_bufs, 0), pick(conv_bufs, 1))
```

```python
import functools
import math

import jax
import jax.numpy as jnp
from jax import lax
from jax.experimental import pallas as pl
from jax.experimental.pallas import tpu as pltpu

F32 = jnp.float32
BF16 = jnp.bfloat16
EPS = 1e-6
ROPE_THETA = 500000.0
LRU_C = 8.0
PEER_TOPK = 16
LANES = 128
SUBLANES = 8
NEG_INF = float("-inf")


def _params(sem, vmem_mb=48):
    return pltpu.CompilerParams(dimension_semantics=sem, vmem_limit_bytes=vmem_mb * 1024 * 1024)


def _gelu(x):
    return 0.5 * x * (1.0 + jnp.tanh(math.sqrt(2.0 / math.pi) * (x + 0.044715 * (x * x * x))))


def _sigmoid(x):
    return 1.0 / (1.0 + jnp.exp(-x))


def _mod_kernel(c_ref, w_ref, b_ref, o_ref):
    c = c_ref[...]
    a = (c * _sigmoid(c)).astype(BF16)
    o_ref[...] = jnp.dot(a, w_ref[...], preferred_element_type=F32) + b_ref[...]


def _mod(c_all, w, b):
    m, k = c_all.shape
    n = w.shape[1]
    tn = 1024
    return pl.pallas_call(
        _mod_kernel,
        grid=(n // tn,),
        in_specs=[pl.BlockSpec((m, k), lambda j: (0, 0)),
                  pl.BlockSpec((k, tn), lambda j: (0, j)),
                  pl.BlockSpec((1, tn), lambda j: (0, j))],
        out_specs=pl.BlockSpec((m, tn), lambda j: (0, j)),
        out_shape=jax.ShapeDtypeStruct((m, n), F32),
        compiler_params=_params(("arbitrary",)),
        name="adaln_mod",
    )(c_all, w, b.reshape(1, n))


def _norm_mod(x3, g_ref, sc_ref, sh_ref):
    ms = jnp.mean(x3 * x3, axis=-1, keepdims=True)
    y = x3 * lax.rsqrt(ms + EPS) * g_ref[...]
    return y * (1.0 + sc_ref[...]) + sh_ref[...]


def _linear_kernel(*refs, n_x, has_norm, has_resid, tm):
    it = iter(refs)
    x_refs = [next(it) for _ in range(n_x)]
    w_refs = [next(it) for _ in range(n_x)]
    if has_norm:
        g_ref, sc_ref, sh_ref = next(it), next(it), next(it)
    if has_resid:
        r_ref, gt_ref = next(it), next(it)
    o_ref = next(it)
    if has_norm:
        xn_ref = next(it)

        @pl.when(pl.program_id(1) == 0)
        def _():
            y = _norm_mod(x_refs[0][...], g_ref, sc_ref, sh_ref)
            xn_ref[...] = y.reshape(tm, y.shape[-1]).astype(BF16)

        acc = jnp.dot(xn_ref[...], w_refs[0][...], preferred_element_type=F32)
    else:
        acc = None
        for x_ref, w_ref in zip(x_refs, w_refs):
            d = jnp.dot(x_ref[...].astype(BF16), w_ref[...], preferred_element_type=F32)
            acc = d if acc is None else acc + d
    y3 = acc.reshape(tm // SUBLANES, SUBLANES, acc.shape[-1])
    if has_resid:
        y3 = r_ref[...] + gt_ref[...] * y3
    o_ref[...] = y3.astype(o_ref.dtype)


def _linear(xs, ws, *, norm=None, resid=None, out_dtype=F32, tm=512, tn=1024, name="linear"):
    has_norm, has_resid = norm is not None, resid is not None
    n = ws[0].shape[1]
    tn = min(tn, n)
    if has_norm:
        t8, _, k = xs[0].shape
        t = t8 * SUBLANES
    else:
        t = xs[0].shape[0]
    assert t % tm == 0 and n % tn == 0
    g8 = tm // SUBLANES
    in_specs, args = [], []
    for x in xs:
        if has_norm:
            in_specs.append(pl.BlockSpec((g8, SUBLANES, x.shape[-1]), lambda i, j: (i, 0, 0)))
        else:
            in_specs.append(pl.BlockSpec((tm, x.shape[-1]), lambda i, j: (i, 0)))
        args.append(x)
    for w in ws:
        in_specs.append(pl.BlockSpec((w.shape[0], tn), lambda i, j: (0, j)))
        args.append(w)
    scratch = []
    if has_norm:
        k = xs[0].shape[-1]
        in_specs += [pl.BlockSpec((1, 1, k), lambda i, j: (0, 0, 0)),
                     pl.BlockSpec((g8, 1, k), lambda i, j: (i, 0, 0)),
                     pl.BlockSpec((g8, 1, k), lambda i, j: (i, 0, 0))]
        args += list(norm)
        scratch.append(pltpu.VMEM((tm, k), BF16))
    if has_resid:
        in_specs += [pl.BlockSpec((g8, SUBLANES, tn), lambda i, j: (i, 0, j)),
                     pl.BlockSpec((g8, 1, tn), lambda i, j: (i, 0, j))]
        args += list(resid)
    return pl.pallas_call(
        functools.partial(_linear_kernel, n_x=len(xs), has_norm=has_norm, has_resid=has_resid, tm=tm),
        grid=(t // tm, n // tn),
        in_specs=in_specs,
        out_specs=pl.BlockSpec((g8, SUBLANES, tn), lambda i, j: (i, 0, j)),
        out_shape=jax.ShapeDtypeStruct((t // SUBLANES, SUBLANES, n), out_dtype),
        scratch_shapes=scratch,
        compiler_params=_params(("parallel", "arbitrary")),
        name=name,
    )(*args)


def _qkprep_kernel(q_ref, k_ref, v_ref, gq_ref, gk_ref, c_ref, s1_ref, s2_ref, p_ref, qo_ref, kv_ref, *,
                   nh, hd, q_scale):
    cos, sin_lo, sin_hi = c_ref[...], s1_ref[...], s2_ref[...]
    ones_blk = p_ref[...]
    rot = hd // 8

    def prep(x, g):
        sq = x * x
        hi = sq.astype(BF16)
        lo = (sq - hi.astype(F32)).astype(BF16)
        ss = (jnp.dot(hi, ones_blk, preferred_element_type=F32)
              + jnp.dot(lo, ones_blk, preferred_element_type=F32))
        y = x * lax.rsqrt(ss * (1.0 / hd) + EPS) * g
        return y * cos + pltpu.roll(y, LANES - rot, 1) * sin_lo + pltpu.roll(y, rot, 1) * sin_hi

    for h in range(nh):
        sl = slice(h * LANES, (h + 1) * LANES)
        qo_ref[:, sl] = (prep(q_ref[:, sl], gq_ref[...]) * q_scale).astype(qo_ref.dtype)
        kv_ref[:, sl] = prep(k_ref[:, sl], gk_ref[...])
    kv_ref[:, nh * LANES:] = v_ref[...]


def _qkprep(proj, gq, gk, cos_t, sin_lo_t, sin_hi_t, ones_blk, *, nh, hd, tm=512):
    t, n = proj.shape
    w = nh * 2 * hd
    assert 2 * hd == LANES and n == 4 * w and t % tm == 0
    row = lambda i: (i, 0)
    full = lambda i: (0, 0)
    return pl.pallas_call(
        functools.partial(_qkprep_kernel, nh=nh, hd=hd, q_scale=hd ** -0.5),
        grid=(t // tm,),
        in_specs=[pl.BlockSpec((tm, w), lambda i: (i, 1)),
                  pl.BlockSpec((tm, w), lambda i: (i, 2)),
                  pl.BlockSpec((tm, w), lambda i: (i, 3)),
                  pl.BlockSpec((1, LANES), full), pl.BlockSpec((1, LANES), full),
                  pl.BlockSpec((tm, LANES), row), pl.BlockSpec((tm, LANES), row), pl.BlockSpec((tm, LANES), row),
                  pl.BlockSpec((LANES, LANES), full)],
        out_specs=[pl.BlockSpec((tm, w), row), pl.BlockSpec((tm, 2 * w), row)],
        out_shape=[jax.ShapeDtypeStruct((t, w), BF16), jax.ShapeDtypeStruct((t, 2 * w), F32)],
        compiler_params=_params(("parallel",)),
        name="qk_prep",
    )(proj, proj, proj, gq, gk, cos_t, sin_lo_t, sin_hi_t, ones_blk)


def _subln(o, g_ref, out_scale):
    return o * lax.rsqrt(jnp.mean(o * o, axis=-1, keepdims=True) + EPS) * g_ref[...] * out_scale


def _online_softmax_step(s, v, m_ref, l_ref, acc_ref, idx):
    m_prev = m_ref[idx]
    m_new = jnp.maximum(m_prev, jnp.max(s, axis=-1, keepdims=True))
    alpha = jnp.exp(m_prev - m_new)
    p = jnp.exp(s - m_new)
    l_ref[idx] = alpha * l_ref[idx] + jnp.sum(p, axis=-1, keepdims=True)
    acc_ref[idx] = alpha * acc_ref[idx] + jnp.dot(p.astype(BF16), v, preferred_element_type=F32)
    m_ref[idx] = m_new


def _flash_kernel(lam_ref, q_ref, k_ref, v_ref, g_ref, o_ref, m_ref, l_ref, acc_ref, *, tq, hd, out_scale):
    qi, ki = pl.program_id(2), pl.program_id(3)

    @pl.when(ki == 0)
    def _():
        m_ref[...] = jnp.full(m_ref.shape, NEG_INF, F32)
        l_ref[...] = jnp.zeros(l_ref.shape, F32)
        acc_ref[...] = jnp.zeros(acc_ref.shape, F32)

    def step(masked):
        q = q_ref[0]
        k = k_ref[0].astype(BF16)
        v = v_ref[0].astype(BF16)
        lane = lax.broadcasted_iota(jnp.int32, q.shape, 1)
        for s in range(2):
            qs = jnp.where((lane >= s * hd) == (lane < (s + 1) * hd), q, jnp.zeros_like(q))
            sc = lax.dot_general(qs, k, (((1,), (1,)), ((), ())), preferred_element_type=F32)
            if masked:
                row = lax.broadcasted_iota(jnp.int32, sc.shape, 0)
                col = lax.broadcasted_iota(jnp.int32, sc.shape, 1)
                sc = jnp.where(col <= row, sc, NEG_INF)
            _online_softmax_step(sc, v, m_ref, l_ref, acc_ref, s)

    @pl.when(ki < qi)
    def _():
        step(False)

    @pl.when(ki == qi)
    def _():
        step(True)
        o = acc_ref[0] / l_ref[0] - lam_ref[...] * (acc_ref[1] / l_ref[1])
        o_ref[0] = _subln(o, g_ref, out_scale).astype(o_ref.dtype)


def _flash_attention(q3, kv3, lam, subln_g, *, nh, hd, out_scale, tq=1024):
    b, l, w = q3.shape
    tq = min(tq, l)
    assert l % tq == 0
    nq = l // tq
    return pl.pallas_call(
        functools.partial(_flash_kernel, tq=tq, hd=hd, out_scale=out_scale),
        grid=(b, nh, nq, nq),
        in_specs=[pl.BlockSpec((1, 1), lambda bb, h, qi, ki: (0, 0)),
                  pl.BlockSpec((1, tq, LANES), lambda bb, h, qi, ki: (bb, qi, h)),
                  pl.BlockSpec((1, tq, LANES), lambda bb, h, qi, ki: (bb, jnp.minimum(ki, qi), h)),
                  pl.BlockSpec((1, tq, LANES), lambda bb, h, qi, ki: (bb, jnp.minimum(ki, qi), nh + h)),
                  pl.BlockSpec((1, LANES), lambda bb, h, qi, ki: (0, 0))],
        out_specs=pl.BlockSpec((1, tq, LANES), lambda bb, h, qi, ki: (bb, qi, h)),
        out_shape=jax.ShapeDtypeStruct((b, l, w), BF16),
        scratch_shapes=[pltpu.VMEM((2, tq, 1), F32), pltpu.VMEM((2, tq, 1), F32), pltpu.VMEM((2, tq, LANES), F32)],
        compiler_params=_params(("parallel", "parallel", "parallel", "arbitrary")),
        name="diff_attn_prompt",
    )(lam, q3, kv3, kv3, subln_g)


def _decode_kernel(pt_ref, lam_ref, q_ref, kvn_ref, *rest, npg, nh, hd, dl, psz, out_scale):
    page_refs = rest[:npg]
    g_ref, o_ref, qw_s, s_s, v_s, m_s = rest[npg:]
    p = pl.program_id(1)
    cols = nh * 2 * dl
    nt = (((1,), (1,)), ((), ()))

    @pl.when(p == 0)
    def _():
        q = q_ref[0].astype(F32)
        lane = lax.broadcasted_iota(jnp.int32, (dl, LANES), 1)
        pieces = []
        for h in range(nh):
            qh = q[:, h * LANES:(h + 1) * LANES]
            for s in range(2):
                pieces.append(jnp.where((lane >= s * hd) == (lane < (s + 1) * hd), qh, 0.0))
        qw_s[...] = jnp.concatenate(pieces, axis=0).astype(BF16)
        m_s[...] = jnp.full(m_s.shape, NEG_INF, F32)

    for i, ref in enumerate(page_refs):
        pg = ref[...].reshape(psz, 2 * nh, LANES)
        k2 = pg[:, :nh, :].reshape(psz * nh, LANES).astype(BF16)
        v2 = pg[:, nh:, :].reshape(psz * nh, LANES).astype(BF16)
        sc = lax.dot_general(k2, qw_s[...], nt, preferred_element_type=F32)
        slot = p * npg + i
        s_s[slot] = sc
        v_s[slot] = v2
        m_s[...] = jnp.maximum(m_s[...], jnp.max(sc.reshape(psz, nh, cols), axis=0))

    @pl.when(p == pl.num_programs(1) - 1)
    def _():
        w = nh * LANES
        kvn = kvn_ref[0]
        pad = jnp.zeros((LANES - nh * dl, LANES), F32)
        kn = jnp.concatenate([kvn[:, h * LANES:(h + 1) * LANES] for h in range(nh)] + [pad], axis=0)
        vn = jnp.concatenate([kvn[:, w + h * LANES:w + (h + 1) * LANES] for h in range(nh)] + [pad], axis=0)
        sn = lax.dot_general(kn.astype(BF16), qw_s[...], nt, preferred_element_type=F32)
        r = lax.broadcasted_iota(jnp.int32, sn.shape, 0)
        c = lax.broadcasted_iota(jnp.int32, sn.shape, 1)
        same_head = r // dl == c // (2 * dl)
        causal = r % dl <= c % dl
        sn_m = jnp.where(same_head, jnp.where(causal, sn, NEG_INF), NEG_INF)
        hr = lax.broadcasted_iota(jnp.int32, (nh, cols), 0)
        hc = lax.broadcasted_iota(jnp.int32, (nh, cols), 1)
        diag = hr == hc // (2 * dl)
        m_col = jnp.maximum(jnp.max(sn_m, axis=0, keepdims=True),
                            jnp.max(jnp.where(diag, m_s[...], NEG_INF), axis=0, keepdims=True))
        pn = jnp.where(same_head, jnp.where(causal, jnp.exp(sn - m_col), 0.0), 0.0)
        l0 = jnp.sum(pn, axis=0, keepdims=True)
        o0 = jnp.dot(pn.T.astype(BF16), vn.astype(BF16), preferred_element_type=F32)

        def page_body(slot, carry):
            l_acc, o_acc = carry
            s3 = s_s[slot].reshape(psz, nh, cols)
            pm = jnp.where(diag[None], jnp.exp(s3 - m_col[None]), 0.0)
            l_acc = l_acc + jnp.sum(jnp.sum(pm, axis=0), axis=0, keepdims=True)
            pt = pm.reshape(psz * nh, cols).T.astype(BF16)
            return l_acc, o_acc + jnp.dot(pt, v_s[slot], preferred_element_type=F32)

        l_row, o = lax.fori_loop(0, s_s.shape[0], page_body, (l0, o0))
        er = lax.broadcasted_iota(jnp.int32, (cols, cols), 0)
        ec = lax.broadcasted_iota(jnp.int32, (cols, cols), 1)
        l_col = jnp.sum(jnp.where(er == ec, jnp.broadcast_to(l_row, (cols, cols)), 0.0), axis=1, keepdims=True)
        o = o / l_col
        for h in range(nh):
            r1 = slice(h * 2 * dl, h * 2 * dl + dl)
            r2 = slice(h * 2 * dl + dl, (h + 1) * 2 * dl)
            oo = o[r1] - lam_ref[...] * o[r2]
            o_ref[0, :, h * LANES:(h + 1) * LANES] = _subln(oo, g_ref, out_scale).astype(o_ref.dtype)


def _decode_attention(q3, kvn3, cache, layer, page_table, lam, subln_g, *, nh, hd, out_scale, npg=8):
    db, dl, w = q3.shape
    n_pages = page_table.shape[1]
    prow = cache.shape[2]
    psz = prow // (2 * nh)
    cols = nh * 2 * dl
    assert n_pages % npg == 0 and dl == SUBLANES and cols == LANES and 2 * hd == LANES

    def page_spec(i):
        return pl.BlockSpec((pl.Squeezed(), pl.Squeezed(), prow, LANES),
                            lambda b, p, pt: (layer, pt[b, p * npg + i], 0, 0))

    grid_spec = pltpu.PrefetchScalarGridSpec(
        num_scalar_prefetch=1,
        grid=(db, n_pages // npg),
        in_specs=[pl.BlockSpec((1, 1), lambda b, p, pt: (0, 0)),
                  pl.BlockSpec((1, dl, w), lambda b, p, pt: (b, 0, 0)),
                  pl.BlockSpec((1, dl, 2 * w), lambda b, p, pt: (b, 0, 0))]
        + [page_spec(i) for i in range(npg)]
        + [pl.BlockSpec((1, LANES), lambda b, p, pt: (0, 0))],
        out_specs=pl.BlockSpec((1, dl, w), lambda b, p, pt: (b, 0, 0)),
        scratch_shapes=[pltpu.VMEM((cols, LANES), BF16),
                        pltpu.VMEM((n_pages, psz * nh, cols), F32),
                        pltpu.VMEM((n_pages, psz * nh, LANES), BF16),
                        pltpu.VMEM((nh, cols), F32)],
    )
    return pl.pallas_call(
        functools.partial(_decode_kernel, npg=npg, nh=nh, hd=hd, dl=dl, psz=psz, out_scale=out_scale),
        grid_spec=grid_spec,
        out_shape=jax.ShapeDtypeStruct((db, dl, w), BF16),
        compiler_params=_params(("parallel", "arbitrary")),
        name="diff_attn_sample",
    )(page_table, lam, q3, kvn3, *([cache] * npg), subln_g)


def _s5_kernel(*refs, n_u, nseq, ts, lc, npk, out3d):
    u_refs = refs[:n_u]
    (wb_ref, wc_ref, d_ref, ar_ref, ai_ref, h0r_ref, h0i_ref, wg_ref, bg_ref,
     y_ref, hr_out, hi_out, xr_s, xi_s, us_s, ys_s, hst_s) = refs[n_u:]
    s = pl.program_id(1)
    nlb = xr_s.shape[0]
    nst = nlb * LANES
    pb = nlb // npk
    pw = pb * LANES
    cw = d_ref.shape[1] // npk
    gb = lc // LANES

    @pl.when(s == 0)
    def _():
        hst_s[0] = h0r_ref[0]
        hst_s[1] = h0i_ref[0]

    u = jnp.concatenate([r[...] for r in u_refs], axis=0) if n_u > 1 else u_refs[0][...]
    for q in range(nseq):
        for b in range(us_s.shape[0]):
            us_s[b, pl.ds(q, ts, stride=nseq), :] = u[q * ts:(q + 1) * ts, b * LANES:(b + 1) * LANES]
    ub = jnp.concatenate([us_s[b] for b in range(us_s.shape[0])], axis=1).astype(BF16)
    for pk in range(npk):
        x = jnp.dot(ub[:, pk * cw:(pk + 1) * cw], wb_ref[pk], preferred_element_type=F32)
        for q in range(pb):
            xr_s[pk * pb + q] = x[:, q * LANES:(q + 1) * LANES]
            xi_s[pk * pb + q] = x[:, pw + q * LANES:pw + (q + 1) * LANES]

    for c in range(nlb // gb):
        blks = list(range(c * gb, (c + 1) * gb))
        ar = [jnp.broadcast_to(ar_ref[:, q * LANES:(q + 1) * LANES], (nseq, LANES)) for q in blks]
        ai = [jnp.broadcast_to(ai_ref[:, q * LANES:(q + 1) * LANES], (nseq, LANES)) for q in blks]

        def body(t, carry, blks=blks, ar=ar, ai=ai):
            idx = pl.ds(pl.multiple_of(t * nseq, nseq), nseq)
            out = []
            for n, q in enumerate(blks):
                hr, hi = carry[2 * n], carry[2 * n + 1]
                nr = ar[n] * hr - ai[n] * hi + xr_s[q, idx, :]
                ni = ar[n] * hi + ai[n] * hr + xi_s[q, idx, :]
                xr_s[q, idx, :] = nr
                xi_s[q, idx, :] = ni
                out += [nr, ni]
            return tuple(out)

        init = []
        for q in blks:
            init += [hst_s[0, :, q * LANES:(q + 1) * LANES], hst_s[1, :, q * LANES:(q + 1) * LANES]]
        fin = lax.fori_loop(0, ts, body, tuple(init), unroll=2)
        for n, q in enumerate(blks):
            hst_s[0, :, q * LANES:(q + 1) * LANES] = fin[2 * n]
            hst_s[1, :, q * LANES:(q + 1) * LANES] = fin[2 * n + 1]

    for pk in range(npk):
        hr_b = jnp.concatenate([xr_s[pk * pb + q] for q in range(pb)], axis=1).astype(BF16)
        hi_b = jnp.concatenate([xi_s[pk * pb + q] for q in range(pb)], axis=1).astype(BF16)
        ys_s[pk] = (jnp.dot(hr_b, wc_ref[0, pk], preferred_element_type=F32)
                    + jnp.dot(hi_b, wc_ref[1, pk], preferred_element_type=F32))
    y_state = jnp.concatenate(
        [jnp.concatenate([ys_s[pk, pl.ds(q, ts, stride=nseq), :] for pk in range(npk)], axis=1)
         for q in range(nseq)], axis=0)
    y = y_state + d_ref[...] * u
    g = _gelu(y)
    out = g * _sigmoid(jnp.dot(g.astype(BF16), wg_ref[...], preferred_element_type=F32) + bg_ref[...])
    if out3d:
        y_ref[...] = out.reshape(y_ref.shape).astype(y_ref.dtype)
    else:
        y_ref[...] = out.astype(y_ref.dtype)

    @pl.when(s == pl.num_programs(1) - 1)
    def _():
        hr_out[0] = hst_s[0]
        hi_out[0] = hst_s[1]


def _s5(proj, h0r, h0i, wts, *, row0, nchunk, nseq, slen, ts, prompt):
    wb, wc, dd, ar, ai, wg, bg = wts
    npk = wb.shape[0]
    ch = dd.shape[1]
    nst = ar.shape[1]
    nsteps = slen // ts
    rows = nseq * ts
    lc = max(LANES, 1024 // (-(-nseq // SUBLANES)))
    if prompt:
        assert nchunk == 1 and row0 == 0
        u_specs = [pl.BlockSpec((ts, ch), lambda c, s, q=q: (q * nsteps + s, 0)) for q in range(nseq)]
        y_spec = pl.BlockSpec((nseq, ts, ch), lambda c, s: (0, s, 0))
        y_shape = jax.ShapeDtypeStruct((nseq, slen, ch), BF16)
    else:
        assert nsteps == 1 and row0 % rows == 0
        u_specs = [pl.BlockSpec((rows, ch), lambda c, s: (row0 // rows + c, 0))]
        y_spec = pl.BlockSpec((rows, ch), lambda c, s: (c, 0))
        y_shape = jax.ShapeDtypeStruct((nchunk * rows, ch), BF16)
    n_u = len(u_specs)
    full2 = lambda c, s: (0, 0)
    st_spec = pl.BlockSpec((1, nseq, nst), lambda c, s: (c, 0, 0))
    return pl.pallas_call(
        functools.partial(_s5_kernel, n_u=n_u, nseq=nseq, ts=ts, lc=lc, npk=npk, out3d=prompt),
        grid=(nchunk, nsteps),
        in_specs=u_specs + [
            pl.BlockSpec(wb.shape, lambda c, s: (0, 0, 0)),
            pl.BlockSpec(wc.shape, lambda c, s: (0, 0, 0, 0)),
            pl.BlockSpec((1, ch), full2), pl.BlockSpec((1, nst), full2), pl.BlockSpec((1, nst), full2),
            st_spec, st_spec,
            pl.BlockSpec(wg.shape, full2), pl.BlockSpec((1, ch), full2)],
        out_specs=[y_spec, st_spec, st_spec],
        out_shape=[y_shape, jax.ShapeDtypeStruct((nchunk, nseq, nst), F32),
                   jax.ShapeDtypeStruct((nchunk, nseq, nst), F32)],
        scratch_shapes=[pltpu.VMEM((nst // LANES, rows, LANES), F32), pltpu.VMEM((nst // LANES, rows, LANES), F32),
                        pltpu.VMEM((ch // LANES, rows, LANES), F32),
                        pltpu.VMEM((npk, rows, ch // npk), F32),
                        pltpu.VMEM((2, nseq, nst), F32)],
        compiler_params=_params(("parallel", "arbitrary")),
        name="s5_prompt" if prompt else "s5_sample",
    )(*([proj] * n_u), wb, wc, dd, ar, ai, h0r, h0i, wg, bg)


def _lru_kernel(*refs, n_x, nseq, ts, lc, nblk, out3d):
    g_refs = refs[:n_x]
    x_refs = refs[n_x:2 * n_x]
    (cw_ref, cb_ref, wa_ref, ba_ref, wx_ref, bx_ref, sp_ref, buf_ref, h0_ref,
     y_ref, buf_out, h_out, a_s, b_s, prev_s, h_s) = refs[2 * n_x:]
    s = pl.program_id(1)
    dr = cb_ref.shape[1]
    bs = dr // nblk
    rows = nseq * ts

    @pl.when(s == 0)
    def _():
        prev_s[...] = buf_ref[0]
        h_s[...] = h0_ref[0]

    if n_x > 1:
        x3 = jnp.stack([r[...] for r in x_refs], axis=0)
        gate = jnp.concatenate([r[...] for r in g_refs], axis=0)
    else:
        x3 = x_refs[0][...].reshape(nseq, ts, dr)
        gate = g_refs[0][...]
    xx = jnp.concatenate([prev_s[...], x3], axis=1)
    kw = cw_ref.shape[0]
    conv = cb_ref[...] + xx[:, SUBLANES - (kw - 1):SUBLANES - (kw - 1) + ts] * cw_ref[0:1, :]
    for tap in range(1, kw):
        off = SUBLANES - (kw - 1) + tap
        conv = conv + xx[:, off:off + ts] * cw_ref[tap:tap + 1, :]
    prev_s[...] = xx[:, ts:ts + SUBLANES]

    c2 = conv.reshape(rows, dr)
    cb16 = c2.astype(BF16)
    rs, is_ = [], []
    for n in range(nblk):
        blk = cb16[:, n * bs:(n + 1) * bs]
        rs.append(jnp.dot(blk, wa_ref[n], preferred_element_type=F32))
        is_.append(jnp.dot(blk, wx_ref[n], preferred_element_type=F32))
    r = _sigmoid(jnp.concatenate(rs, axis=1) + ba_ref[...])
    i = _sigmoid(jnp.concatenate(is_, axis=1) + bx_ref[...])
    log_a = -LRU_C * r * sp_ref[...]
    a_all = jnp.exp(log_a)
    b_all = jnp.sqrt(1.0 - jnp.exp(2.0 * log_a)) * (i * c2)
    nlb = dr // LANES
    for q in range(nlb):
        a_s[q] = a_all[:, q * LANES:(q + 1) * LANES]
        b_s[q] = b_all[:, q * LANES:(q + 1) * LANES]

    gb = lc // LANES
    for c in range(nlb // gb):
        blks = list(range(c * gb, (c + 1) * gb))

        def body(t, carry, blks=blks):
            idx = pl.ds(t, nseq, stride=ts)
            out = []
            for n, q in enumerate(blks):
                h = a_s[q, idx, :] * carry[n] + b_s[q, idx, :]
                b_s[q, idx, :] = h
                out.append(h)
            return tuple(out)

        fin = lax.fori_loop(0, ts, body, tuple(h_s[:, q * LANES:(q + 1) * LANES] for q in blks))
        for n, q in enumerate(blks):
            h_s[:, q * LANES:(q + 1) * LANES] = fin[n]

    out = _gelu(gate) * jnp.concatenate([b_s[q] for q in range(nlb)], axis=1)
    if out3d:
        y_ref[...] = out.reshape(y_ref.shape).astype(y_ref.dtype)
    else:
        y_ref[...] = out.astype(y_ref.dtype)

    @pl.when(s == pl.num_programs(1) - 1)
    def _():
        buf_out[0] = prev_s[...]
        h_out[0] = h_s[...]


def _lru(proj, buf8, h0, wts, *, row0, nchunk, nseq, slen, ts, prompt):
    cw, cb, wa, ba, wx, bx, sp = wts
    dr = cb.shape[1]
    nblk = wa.shape[0]
    nsteps = slen // ts
    rows = nseq * ts
    lc = dr if nseq <= SUBLANES else 256
    assert dr % lc == 0
    if prompt:
        assert nchunk == 1 and row0 == 0
        g_specs = [pl.BlockSpec((ts, dr), lambda c, s, q=q: (q * nsteps + s, 0)) for q in range(nseq)]
        x_specs = [pl.BlockSpec((ts, dr), lambda c, s, q=q: (q * nsteps + s, 1)) for q in range(nseq)]
        y_spec = pl.BlockSpec((nseq, ts, dr), lambda c, s: (0, s, 0))
        y_shape = jax.ShapeDtypeStruct((nseq, slen, dr), BF16)
    else:
        assert nsteps == 1 and row0 % rows == 0 and ts == SUBLANES
        g_specs = [pl.BlockSpec((rows, dr), lambda c, s: (row0 // rows + c, 0))]
        x_specs = [pl.BlockSpec((rows, dr), lambda c, s: (row0 // rows + c, 1))]
        y_spec = pl.BlockSpec((rows, dr), lambda c, s: (c, 0))
        y_shape = jax.ShapeDtypeStruct((nchunk * rows, dr), BF16)
    n_x = len(x_specs)
    full2 = lambda c, s: (0, 0)
    full3 = lambda c, s: (0, 0, 0)
    return pl.pallas_call(
        functools.partial(_lru_kernel, n_x=n_x, nseq=nseq, ts=ts, lc=lc, nblk=nblk, out3d=prompt),
        grid=(nchunk, nsteps),
        in_specs=g_specs + x_specs + [
            pl.BlockSpec(cw.shape, full2), pl.BlockSpec((1, dr), full2),
            pl.BlockSpec(wa.shape, full3), pl.BlockSpec((1, dr), full2),
            pl.BlockSpec(wx.shape, full3), pl.BlockSpec((1, dr), full2),
            pl.BlockSpec((1, dr), full2),
            pl.BlockSpec((1, nseq, SUBLANES, dr), lambda c, s: (c, 0, 0, 0)),
            pl.BlockSpec((1, nseq, dr), lambda c, s: (c, 0, 0))],
        out_specs=[y_spec,
                   pl.BlockSpec((1, nseq, SUBLANES, dr), lambda c, s: (c, 0, 0, 0)),
                   pl.BlockSpec((1, nseq, dr), lambda c, s: (c, 0, 0))],
        out_shape=[y_shape, jax.ShapeDtypeStruct((nchunk, nseq, SUBLANES, dr), F32),
                   jax.ShapeDtypeStruct((nchunk, nseq, dr), F32)],
        scratch_shapes=[pltpu.VMEM((dr // LANES, rows, LANES), F32), pltpu.VMEM((dr // LANES, rows, LANES), F32),
                        pltpu.VMEM((nseq, SUBLANES, dr), F32), pltpu.VMEM((nseq, dr), F32)],
        compiler_params=_params(("parallel", "arbitrary")),
        name="rglru_prompt" if prompt else "rglru_sample",
    )(*([proj] * (2 * n_x)), cw, cb, wa, ba, wx, bx, sp, buf8, h0)


def _peer_scores_kernel(x_ref, g_ref, sc_ref, sh_ref, wq_ref, k1_ref, k2_ref, hf_ref, s1_ref, s2_ref, *,
                        tm, nh, dk):
    y = _norm_mod(x_ref[...], g_ref, sc_ref, sh_ref)
    hf = y.reshape(tm, y.shape[-1]).astype(BF16)
    hf_ref[...] = hf
    q = jnp.dot(hf, wq_ref[...], preferred_element_type=F32)
    half = dk // 2
    q1, q2 = [], []
    for h in range(nh):
        qh = q[:, h * dk:(h + 1) * dk]
        qn = qh * lax.rsqrt(jnp.mean(qh * qh, axis=-1, keepdims=True) + EPS)
        q1.append(qn[:, :half])
        q2.append(qn[:, half:])
    nt = (((1,), (1,)), ((), ()))
    s1_ref[...] = lax.dot_general(k1_ref[...], jnp.concatenate(q1, axis=1).astype(BF16), nt,
                                  preferred_element_type=F32)
    s2_ref[...] = lax.dot_general(k2_ref[...], jnp.concatenate(q2, axis=1).astype(BF16), nt,
                                  preferred_element_type=F32)


def _peer_scores(x3, norm, wq, kexp1, kexp2, *, nh, tm=512):
    t8, _, d = x3.shape
    t = t8 * SUBLANES
    assert t % tm == 0
    g8 = tm // SUBLANES
    nq = wq.shape[1]
    nrow = kexp1.shape[0]
    full2 = lambda i: (0, 0)
    return pl.pallas_call(
        functools.partial(_peer_scores_kernel, tm=tm, nh=nh, dk=nq // nh),
        grid=(t // tm,),
        in_specs=[pl.BlockSpec((g8, SUBLANES, d), lambda i: (i, 0, 0)),
                  pl.BlockSpec((1, 1, d), lambda i: (0, 0, 0)),
                  pl.BlockSpec((g8, 1, d), lambda i: (i, 0, 0)),
                  pl.BlockSpec((g8, 1, d), lambda i: (i, 0, 0)),
                  pl.BlockSpec(wq.shape, full2), pl.BlockSpec(kexp1.shape, full2), pl.BlockSpec(kexp2.shape, full2)],
        out_specs=[pl.BlockSpec((tm, d), lambda i: (i, 0)),
                   pl.BlockSpec((nrow, tm), lambda i: (0, i)),
                   pl.BlockSpec((nrow, tm), lambda i: (0, i))],
        out_shape=[jax.ShapeDtypeStruct((t, d), BF16), jax.ShapeDtypeStruct((nrow, t), F32),
                   jax.ShapeDtypeStruct((nrow, t), F32)],
        compiler_params=_params(("parallel",)),
        name="peer_scores",
    )(x3, *norm, wq, kexp1, kexp2)


def _hyperbola(k):
    return [(a, b) for a in range(k) for b in range(k) if (a + 1) * (b + 1) <= k]


def _peer_topk_kernel(s1_ref, s2_ref, e1_ref, e2_ref, g_ref, x_s, v_s, i_s, *, nkeys, nh, topk):
    tl = s1_ref.shape[1]
    kid = lax.broadcasted_iota(jnp.int32, (nkeys, nh, tl), 0)

    for side, s_ref in enumerate((s1_ref, s2_ref)):
        x_s[...] = s_ref[...].reshape(nkeys, nh, tl)

        def body(it, carry, side=side):
            x = x_s[...]
            m = jnp.max(x, axis=0)
            idx = jnp.min(jnp.where(x == m[None], kid, nkeys), axis=0)
            v_s[side, it] = m
            i_s[side, it] = idx
            x_s[...] = jnp.where(kid == idx[None], NEG_INF, x)
            return carry

        lax.fori_loop(0, topk, body, 0)

    pairs = _hyperbola(topk)
    v1 = [v_s[0, k] for k in range(topk)]
    v2 = [v_s[1, k] for k in range(topk)]
    cand = [v1[a] + v2[b] for a, b in pairs]
    flat = [a * topk + b for a, b in pairs]
    big = topk * topk
    shift = topk.bit_length() - 1
    assert 1 << shift == topk
    tops, k1s, k2s = [], [], []
    for _ in range(topk):
        m = functools.reduce(jnp.maximum, cand)
        sel = functools.reduce(jnp.minimum, [jnp.where(c == m, f, big) for c, f in zip(cand, flat)])
        cand = [jnp.where(sel == f, NEG_INF, c) for c, f in zip(cand, flat)]
        tops.append(m)
        k1s.append(lax.shift_right_logical(sel, shift))
        k2s.append(sel & (topk - 1))
    i1 = [i_s[0, k] for k in range(topk)]
    i2 = [i_s[1, k] for k in range(topk)]
    ex = [jnp.exp(tv - tops[0]) for tv in tops]
    den = functools.reduce(jnp.add, ex)
    e1, e2, gs = [], [], []
    for k in range(topk):
        e1.append(functools.reduce(jnp.add, [jnp.where(k1s[k] == j, i1[j], 0) for j in range(topk)]))
        e2.append(functools.reduce(jnp.add, [jnp.where(k2s[k] == j, i2[j], 0) for j in range(topk)]))
        gs.append(ex[k] / den)
    e1_ref[...] = jnp.concatenate(e1, axis=0).astype(F32).T
    e2_ref[...] = jnp.concatenate(e2, axis=0).astype(F32).T
    g_ref[...] = jnp.concatenate(gs, axis=0).T


def _peer_topk(s1, s2, *, nkeys, nh, topk=PEER_TOPK, tl=LANES):
    nrow, t = s1.shape
    assert nrow == nkeys * nh and t % tl == 0 and nh * topk == LANES
    in_spec = pl.BlockSpec((nrow, tl), lambda i: (0, i))
    out_spec = pl.BlockSpec((tl, nh * topk), lambda i: (i, 0))
    out = jax.ShapeDtypeStruct((t, nh * topk), F32)
    return pl.pallas_call(
        functools.partial(_peer_topk_kernel, nkeys=nkeys, nh=nh, topk=topk),
        grid=(t // tl,),
        in_specs=[in_spec, in_spec],
        out_specs=[out_spec, out_spec, out_spec],
        out_shape=[out, out, out],
        scratch_shapes=[pltpu.VMEM((nkeys, nh, tl), F32), pltpu.VMEM((2, topk, nh, tl), F32),
                        pltpu.VMEM((2, topk, nh, tl), jnp.int32)],
        compiler_params=_params(("parallel",)),
        name="peer_topk",
    )(s1, s2)


def _peer_dense_kernel(hf_ref, e1_ref, e2_ref, g_ref, u_ref, v_ref, r_ref, gt_ref, o_ref, g3_s, acc_s, w_s, *,
                       tb, nkeys, eb1):
    j = pl.program_id(1)

    @pl.when(j == 0)
    def _():
        acc_s[...] = jnp.zeros(acc_s.shape, F32)
        kio = lax.broadcasted_iota(jnp.int32, (nkeys, LANES), 0).astype(F32)
        nt = (((1,), (1,)), ((), ()))

        def build(t, carry):
            e1 = jnp.broadcast_to(e1_ref[pl.ds(t, 1), :], (nkeys, LANES))
            e2 = jnp.broadcast_to(e2_ref[pl.ds(t, 1), :], (nkeys, LANES))
            gg = jnp.broadcast_to(g_ref[pl.ds(t, 1), :], (nkeys, LANES))
            a = jnp.where(kio == e1, gg, 0.0)
            a_hi = a.astype(BF16)
            a_lo = (a - a_hi.astype(F32)).astype(BF16)
            b = jnp.where(kio == e2, 1.0, 0.0).astype(BF16)
            gmat = lax.dot_general(jnp.concatenate([a_hi, a_lo], axis=1), jnp.concatenate([b, b], axis=1), nt,
                                   preferred_element_type=F32)
            g3_s[pl.ds(pl.multiple_of(t * nkeys, nkeys), nkeys), :] = gmat
            return carry

        lax.fori_loop(0, tb, build, 0, unroll=8)

    x = hf_ref[...]
    cw = 2 * nkeys
    for c in range(eb1 // 2):
        z = lax.dot_general(x, u_ref[c * cw:(c + 1) * cw, :], (((1,), (1,)), ((), ())),
                            preferred_element_type=F32)
        gate = jnp.concatenate(
            [g3_s[pl.ds(j * eb1 + 2 * c + a, tb, stride=nkeys), :] for a in range(2)], axis=1)
        w_s[:, c * cw:(c + 1) * cw] = (gate * _gelu(z)).astype(BF16)
    acc_s[...] += jnp.dot(w_s[...], v_ref[...], preferred_element_type=F32)

    @pl.when(j == pl.num_programs(1) - 1)
    def _():
        o_ref[...] = r_ref[...] + gt_ref[...] * acc_s[...].reshape(o_ref.shape)


def _peer_dense(hf, e1, e2, g, u_tab, v_tab, resid, gate, *, nkeys, tb=256, eb1=16):
    t, d = hf.shape
    ne = u_tab.shape[0]
    assert ne == nkeys * nkeys and t % tb == 0 and nkeys % eb1 == 0 and eb1 % 2 == 0
    eb = eb1 * nkeys
    g8 = tb // SUBLANES
    return pl.pallas_call(
        functools.partial(_peer_dense_kernel, tb=tb, nkeys=nkeys, eb1=eb1),
        grid=(t // tb, ne // eb),
        in_specs=[pl.BlockSpec((tb, d), lambda i, j: (i, 0)),
                  pl.BlockSpec((tb, LANES), lambda i, j: (i, 0)),
                  pl.BlockSpec((tb, LANES), lambda i, j: (i, 0)),
                  pl.BlockSpec((tb, LANES), lambda i, j: (i, 0)),
                  pl.BlockSpec((eb, d), lambda i, j: (j, 0)),
                  pl.BlockSpec((eb, d), lambda i, j: (j, 0)),
                  pl.BlockSpec((g8, SUBLANES, d), lambda i, j: (i, 0, 0)),
                  pl.BlockSpec((g8, 1, d), lambda i, j: (i, 0, 0))],
        out_specs=pl.BlockSpec((g8, SUBLANES, d), lambda i, j: (i, 0, 0)),
        out_shape=jax.ShapeDtypeStruct((t // SUBLANES, SUBLANES, d), F32),
        scratch_shapes=[pltpu.VMEM((tb * nkeys, LANES), F32), pltpu.VMEM((tb, d), F32),
                        pltpu.VMEM((tb, eb), BF16)],
        compiler_params=_params(("parallel", "arbitrary"), vmem_mb=56),
        name="peer_dense",
    )(hf, e1, e2, g, u_tab, v_tab, resid, gate)


def _s5_weights(lam_re, lam_im, b_re, b_im, c_re, c_im, d, log_dt, w_glu, b_glu, *, gpp=8):
    ng, ns = lam_re.shape
    p = d.shape[1]
    dt = jnp.exp(log_dt)[:, None]
    mag = jnp.exp(lam_re * dt)
    ab_re, ab_im = mag * jnp.cos(lam_im * dt), mag * jnp.sin(lam_im * dt)
    den = lam_re * lam_re + lam_im * lam_im
    fr = ((ab_re - 1.0) * lam_re + ab_im * lam_im) / den
    fi = (ab_im * lam_re - (ab_re - 1.0) * lam_im) / den
    xb_re = fr[..., None] * b_re - fi[..., None] * b_im
    xb_im = fr[..., None] * b_im + fi[..., None] * b_re
    npk = ng // gpp
    eye = jnp.eye(gpp, dtype=F32)

    def pack_b(w):
        w = w.reshape(npk, gpp, ns, p)
        return jnp.einsum("kgnp,gh->kgphn", w, eye).reshape(npk, gpp * p, gpp * ns)

    def pack_c(w):
        w = w.reshape(npk, gpp, p, ns)
        return jnp.einsum("kgpn,gh->kgnhp", w, eye).reshape(npk, gpp * ns, gpp * p)

    wb = jnp.concatenate([pack_b(xb_re), pack_b(xb_im)], axis=2).astype(BF16)
    wc = jnp.stack([pack_c(c_re), -pack_c(c_im)], axis=0).astype(BF16)
    return (wb, wc, d.reshape(1, ng * p), ab_re.reshape(1, ng * ns), ab_im.reshape(1, ng * ns),
            w_glu.astype(BF16), b_glu.reshape(1, -1))


def _rope_tables(pos, hd):
    rot = hd // 4
    inv = jnp.power(ROPE_THETA, -jnp.arange(0, rot, 2, dtype=F32) / rot)
    ang = pos.astype(F32)[:, None] * inv[None, :]
    cos, sin = jnp.cos(ang), jnp.sin(ang)
    n = pos.shape[0]
    half = rot // 2
    ones = jnp.ones((n, hd - rot), F32)
    zeros = jnp.zeros((n, hd - rot), F32)
    zh = jnp.zeros((n, half), F32)
    cos_t = jnp.concatenate([cos, cos, ones], axis=1)
    sin_lo = jnp.concatenate([-sin, zh, zeros], axis=1)
    sin_hi = jnp.concatenate([zh, sin, zeros], axis=1)
    rep = LANES // hd
    return jnp.tile(cos_t, (1, rep)), jnp.tile(sin_lo, (1, rep)), jnp.tile(sin_hi, (1, rep))


def kernel(x_prompt, x_sample, cache_kv, page_table, state_s5_re, state_s5_im, state_lru, state_conv, c_prompt, c_sample, norm_mix, norm_ffn, w_ada, b_ada, w_in_even, w_out_even, q_norm, k_norm, lambda_q1, lambda_k1, lambda_q2, lambda_k2, attn_subln, s5_lambda_re, s5_lambda_im, s5_b_re, s5_b_im, s5_c_re, s5_c_im, s5_d, s5_log_dt, s5_w_glu, s5_b_glu, w_in_odd, conv_w, conv_b, lru_w_a, lru_b_a, lru_w_x, lru_b_x, lru_lambda, w_out_odd, peer_wq, peer_keys, peer_u, peer_v):
    nb, sl, dm = x_prompt.shape
    db, dl, _ = x_sample.shape
    depth = norm_mix.shape[0]
    n_even = w_in_even.shape[0]
    nh = cache_kv.shape[4]
    hd = cache_kv.shape[5] // 2
    psz = cache_kv.shape[2]
    n_pool = cache_kv.shape[1]
    qkv_w = nh * 2 * hd
    s5_ch = w_in_even.shape[2] - 3 * qkv_w
    ng, ns = s5_lambda_re.shape[1], s5_lambda_re.shape[2]
    dr = conv_b.shape[1]
    kw = conv_w.shape[1]
    p_heads, _, nkeys, _ = peer_keys.shape[1:]
    tp, tsmp = nb * sl, db * dl
    t = tp + tsmp
    t8 = t // SUBLANES
    past_len = page_table.shape[1] * psz

    x = jnp.concatenate([x_prompt.reshape(tp, dm), x_sample.reshape(tsmp, dm)], axis=0).reshape(t8, SUBLANES, dm)
    c_all = jnp.concatenate([c_prompt, c_sample], axis=0)

    def expand(m):
        return jnp.concatenate([jnp.repeat(m[:nb], sl // SUBLANES, axis=0),
                                jnp.repeat(m[nb:], dl // SUBLANES, axis=0)], axis=0)[:, None, :]

    pos = jnp.concatenate([jnp.tile(jnp.arange(sl, dtype=jnp.int32), nb),
                           jnp.tile(past_len + jnp.arange(dl, dtype=jnp.int32), db)])
    cos_t, sin_lo, sin_hi = _rope_tables(pos, hd)
    lane = jnp.arange(LANES)
    ones_blk = (lane[:, None] // hd == lane[None, :] // hd).astype(BF16)
    eye_h = jnp.eye(p_heads, dtype=F32)
    cache_rows = cache_kv.reshape(n_even, n_pool, psz * 2 * nh, 2 * hd)
    rb = 32 if db % 32 == 0 else db
    ts_p = 64 if sl % 64 == 0 else sl

    kv_rows, s5_re, s5_im, lru_h, conv_bufs = [], [], [], [], []
    for layer in range(depth):
        j = layer // 2
        mod = _mod(c_all, w_ada[layer].astype(BF16), b_ada[layer])
        sh1, sc1, g1, sh2, sc2, g2 = [expand(m) for m in jnp.split(mod, 6, axis=-1)]
        nm = (norm_mix[layer].reshape(1, 1, dm), sc1, sh1)
        if layer % 2 == 0:
            proj = _linear([x], [w_in_even[j].astype(BF16)], norm=nm, name="in_proj_even").reshape(t, -1)
            tile2 = lambda g: jnp.tile(g, LANES // hd).reshape(1, LANES)
            qn, kv = _qkprep(proj, tile2(q_norm[j]), tile2(k_norm[j]), cos_t, sin_lo, sin_hi, ones_blk, nh=nh, hd=hd)
            kv_rows.append(kv)
            lam_init = 0.8 - 0.6 * math.exp(-0.3 * layer)
            lam = (jnp.exp(jnp.sum(lambda_q1[j] * lambda_k1[j])) - jnp.exp(jnp.sum(lambda_q2[j] * lambda_k2[j]))
                   + lam_init).reshape(1, 1)
            sub_g = attn_subln[j].reshape(1, LANES)
            o_p = _flash_attention(qn[:tp].reshape(nb, sl, qkv_w), kv[:tp].reshape(nb, sl, 2 * qkv_w), lam, sub_g,
                                   nh=nh, hd=hd, out_scale=1.0 - lam_init)
            o_s = _decode_attention(qn[tp:].reshape(db, dl, qkv_w), kv[tp:].reshape(db, dl, 2 * qkv_w), cache_rows,
                                    j, page_table, lam, sub_g, nh=nh, hd=hd, out_scale=1.0 - lam_init)
            wts = _s5_weights(s5_lambda_re[j], s5_lambda_im[j], s5_b_re[j], s5_b_im[j], s5_c_re[j], s5_c_im[j],
                              s5_d[j], s5_log_dt[j], s5_w_glu[j], s5_b_glu[j])
            z_s5 = jnp.zeros((1, nb, ng * ns), F32)
            y_p, hr_p, hi_p = _s5(proj, z_s5, z_s5, wts, row0=0, nchunk=1, nseq=nb, slen=sl, ts=ts_p, prompt=True)
            y_s, hr_s, hi_s = _s5(proj, state_s5_re[j].reshape(db // rb, rb, ng * ns),
                                  state_s5_im[j].reshape(db // rb, rb, ng * ns), wts,
                                  row0=tp, nchunk=db // rb, nseq=rb, slen=dl, ts=dl, prompt=False)
            s5_re.append((hr_p.reshape(nb, ng, ns), hr_s.reshape(db, ng, ns)))
            s5_im.append((hi_p.reshape(nb, ng, ns), hi_s.reshape(db, ng, ns)))
            y_mix = jnp.concatenate([y_p.reshape(tp, s5_ch), y_s], axis=0)
            o_mix = jnp.concatenate([o_p.reshape(tp, qkv_w), o_s.reshape(tsmp, qkv_w)], axis=0)
            w_out = w_out_even[j].astype(BF16)
            x = _linear([y_mix, o_mix], [w_out[:s5_ch], w_out[s5_ch:]], resid=(x, g1), name="out_proj_even")
        else:
            proj = _linear([x], [w_in_odd[j].astype(BF16)], norm=nm, tn=dr, name="in_proj_odd").reshape(t, -1)
            sp = jax.nn.softplus(-lru_lambda[j]).reshape(1, dr)
            wts = (conv_w[j], conv_b[j].reshape(1, dr), lru_w_a[j].astype(BF16), lru_b_a[j].reshape(1, dr),
                   lru_w_x[j].astype(BF16), lru_b_x[j].reshape(1, dr), sp)
            pad = SUBLANES - (kw - 1)
            y_p, buf_p, h_p = _lru(proj, jnp.zeros((1, nb, SUBLANES, dr), F32), jnp.zeros((1, nb, dr), F32), wts,
                                   row0=0, nchunk=1, nseq=nb, slen=sl, ts=ts_p, prompt=True)
            buf_s0 = jnp.pad(state_conv[j], ((0, 0), (pad, 0), (0, 0))).reshape(db // rb, rb, SUBLANES, dr)
            y_s, buf_s, h_s = _lru(proj, buf_s0, state_lru[j].reshape(db // rb, rb, dr), wts,
                                   row0=tp, nchunk=db // rb, nseq=rb, slen=dl, ts=dl, prompt=False)
            lru_h.append((h_p.reshape(nb, dr), h_s.reshape(db, dr)))
            conv_bufs.append((buf_p.reshape(nb, SUBLANES, dr)[:, pad:], buf_s.reshape(db, SUBLANES, dr)[:, pad:]))
            y_mix = jnp.concatenate([y_p.reshape(tp, dr), y_s], axis=0)
            x = _linear([y_mix], [w_out_odd[j].astype(BF16)], resid=(x, g1), name="out_proj_odd")

        keys = peer_keys[layer]
        kexp = [jnp.einsum("hkd,hg->khgd", keys[:, s], eye_h).reshape(nkeys * p_heads, -1).astype(BF16)
                for s in range(2)]
        hf, s1, s2 = _peer_scores(x, (norm_ffn[layer].reshape(1, 1, dm), sc2, sh2), peer_wq[layer].astype(BF16),
                                  kexp[0], kexp[1], nh=p_heads)
        e1, e2, gates = _peer_topk(s1, s2, nkeys=nkeys, nh=p_heads)
        x = _peer_dense(hf, e1, e2, gates, peer_u[layer].astype(BF16), peer_v[layer].astype(BF16), x, g2,
                        nkeys=nkeys)

    x2 = x.reshape(t, dm)
    y_prompt = x2[:tp].reshape(nb, sl, dm)
    y_sample = x2[tp:].reshape(db, dl, dm)
    kv_all = jnp.stack(kv_rows)
    kv_prompt = kv_all[:, :tp].reshape(n_even, nb, sl, 2, nh, 2 * hd)
    kv_sample = kv_all[:, tp:].reshape(n_even, db, dl, 2, nh, 2 * hd)
    pick = lambda lst, i: jnp.stack([e[i] for e in lst])
    return (y_prompt, y_sample, kv_prompt, kv_sample, pick(s5_re, 0), pick(s5_im, 0), pick(s5_re, 1),
            pick(s5_im, 1), pick(lru_h, 0), pick(lru_h, 1), pick(conv_bufs, 0), pick(conv_bufs, 1))
```

```python
import functools
import math

import jax
import jax.numpy as jnp
from jax import lax
from jax.experimental import pallas as pl
from jax.experimental.pallas import tpu as pltpu

F32 = jnp.float32
BF16 = jnp.bfloat16
EPS = 1e-6
ROPE_THETA = 500000.0
LRU_C = 8.0
PEER_TOPK = 16
LANES = 128
SUBLANES = 8
NEG_INF = float("-inf")


def _params(sem, vmem_mb=48):
    return pltpu.CompilerParams(dimension_semantics=sem, vmem_limit_bytes=vmem_mb * 1024 * 1024)


def _gelu(x):
    return 0.5 * x * (1.0 + jnp.tanh(math.sqrt(2.0 / math.pi) * (x + 0.044715 * (x * x * x))))


def _sigmoid(x):
    return 1.0 / (1.0 + jnp.exp(-x))


def _mod_kernel(c_ref, w_ref, b_ref, o_ref):
    c = c_ref[...]
    a = (c * _sigmoid(c)).astype(BF16)
    o_ref[...] = jnp.dot(a, w_ref[...], preferred_element_type=F32) + b_ref[...]


def _mod(c_all, w, b):
    m, k = c_all.shape
    n = w.shape[1]
    tn = 1024
    return pl.pallas_call(
        _mod_kernel,
        grid=(n // tn,),
        in_specs=[pl.BlockSpec((m, k), lambda j: (0, 0)),
                  pl.BlockSpec((k, tn), lambda j: (0, j)),
                  pl.BlockSpec((1, tn), lambda j: (0, j))],
        out_specs=pl.BlockSpec((m, tn), lambda j: (0, j)),
        out_shape=jax.ShapeDtypeStruct((m, n), F32),
        compiler_params=_params(("arbitrary",)),
        name="adaln_mod",
    )(c_all, w, b.reshape(1, n))


def _norm_mod(x3, g_ref, sc_ref, sh_ref):
    ms = jnp.mean(x3 * x3, axis=-1, keepdims=True)
    y = x3 * lax.rsqrt(ms + EPS) * g_ref[...]
    return y * (1.0 + sc_ref[...]) + sh_ref[...]


def _linear_kernel(*refs, n_x, has_norm, has_resid, tm):
    it = iter(refs)
    x_refs = [next(it) for _ in range(n_x)]
    w_refs = [next(it) for _ in range(n_x)]
    if has_norm:
        g_ref, sc_ref, sh_ref = next(it), next(it), next(it)
    if has_resid:
        r_ref, gt_ref = next(it), next(it)
    o_ref = next(it)
    if has_norm:
        xn_ref = next(it)

        @pl.when(pl.program_id(1) == 0)
        def _():
            y = _norm_mod(x_refs[0][...], g_ref, sc_ref, sh_ref)
            xn_ref[...] = y.reshape(tm, y.shape[-1]).astype(BF16)

        acc = jnp.dot(xn_ref[...], w_refs[0][...], preferred_element_type=F32)
    else:
        acc = None
        for x_ref, w_ref in zip(x_refs, w_refs):
            d = jnp.dot(x_ref[...].astype(BF16), w_ref[...], preferred_element_type=F32)
            acc = d if acc is None else acc + d
    y3 = acc.reshape(tm // SUBLANES, SUBLANES, acc.shape[-1])
    if has_resid:
        y3 = r_ref[...] + gt_ref[...] * y3
    o_ref[...] = y3.astype(o_ref.dtype)


def _linear(xs, ws, *, norm=None, resid=None, out_dtype=F32, tm=512, tn=1024, name="linear"):
    has_norm, has_resid = norm is not None, resid is not None
    n = ws[0].shape[1]
    tn = min(tn, n)
    if has_norm:
        t8, _, k = xs[0].shape
        t = t8 * SUBLANES
    else:
        t = xs[0].shape[0]
    assert t % tm == 0 and n % tn == 0
    g8 = tm // SUBLANES
    in_specs, args = [], []
    for x in xs:
        if has_norm:
            in_specs.append(pl.BlockSpec((g8, SUBLANES, x.shape[-1]), lambda i, j: (i, 0, 0)))
        else:
            in_specs.append(pl.BlockSpec((tm, x.shape[-1]), lambda i, j: (i, 0)))
        args.append(x)
    for w in ws:
        in_specs.append(pl.BlockSpec((w.shape[0], tn), lambda i, j: (0, j)))
        args.append(w)
    scratch = []
    if has_norm:
        k = xs[0].shape[-1]
        in_specs += [pl.BlockSpec((1, 1, k), lambda i, j: (0, 0, 0)),
                     pl.BlockSpec((g8, 1, k), lambda i, j: (i, 0, 0)),
                     pl.BlockSpec((g8, 1, k), lambda i, j: (i, 0, 0))]
        args += list(norm)
        scratch.append(pltpu.VMEM((tm, k), BF16))
    if has_resid:
        in_specs += [pl.BlockSpec((g8, SUBLANES, tn), lambda i, j: (i, 0, j)),
                     pl.BlockSpec((g8, 1, tn), lambda i, j: (i, 0, j))]
        args += list(resid)
    return pl.pallas_call(
        functools.partial(_linear_kernel, n_x=len(xs), has_norm=has_norm, has_resid=has_resid, tm=tm),
        grid=(t // tm, n // tn),
        in_specs=in_specs,
        out_specs=pl.BlockSpec((g8, SUBLANES, tn), lambda i, j: (i, 0, j)),
        out_shape=jax.ShapeDtypeStruct((t // SUBLANES, SUBLANES, n), out_dtype),
        scratch_shapes=scratch,
        compiler_params=_params(("parallel", "arbitrary")),
        name=name,
    )(*args)


def _qkprep_kernel(q_ref, k_ref, v_ref, gq_ref, gk_ref, c_ref, s1_ref, s2_ref, p_ref, qo_ref, kv_ref, *,
                   nh, hd, q_scale):
    cos, sin_lo, sin_hi = c_ref[...], s1_ref[...], s2_ref[...]
    ones_blk = p_ref[...]
    rot = hd // 8

    def prep(x, g):
        sq = x * x
        hi = sq.astype(BF16)
        lo = (sq - hi.astype(F32)).astype(BF16)
        ss = (jnp.dot(hi, ones_blk, preferred_element_type=F32)
              + jnp.dot(lo, ones_blk, preferred_element_type=F32))
        y = x * lax.rsqrt(ss * (1.0 / hd) + EPS) * g
        return y * cos + pltpu.roll(y, LANES - rot, 1) * sin_lo + pltpu.roll(y, rot, 1) * sin_hi

    for h in range(nh):
        sl = slice(h * LANES, (h + 1) * LANES)
        qo_ref[:, sl] = (prep(q_ref[:, sl], gq_ref[...]) * q_scale).astype(qo_ref.dtype)
        kv_ref[:, sl] = prep(k_ref[:, sl], gk_ref[...])
    kv_ref[:, nh * LANES:] = v_ref[...]


def _qkprep(proj, gq, gk, cos_t, sin_lo_t, sin_hi_t, ones_blk, *, nh, hd, tm=512):
    t, n = proj.shape
    w = nh * 2 * hd
    assert 2 * hd == LANES and n == 4 * w and t % tm == 0
    row = lambda i: (i, 0)
    full = lambda i: (0, 0)
    return pl.pallas_call(
        functools.partial(_qkprep_kernel, nh=nh, hd=hd, q_scale=hd ** -0.5),
        grid=(t // tm,),
        in_specs=[pl.BlockSpec((tm, w), lambda i: (i, 1)),
                  pl.BlockSpec((tm, w), lambda i: (i, 2)),
                  pl.BlockSpec((tm, w), lambda i: (i, 3)),
                  pl.BlockSpec((1, LANES), full), pl.BlockSpec((1, LANES), full),
                  pl.BlockSpec((tm, LANES), row), pl.BlockSpec((tm, LANES), row), pl.BlockSpec((tm, LANES), row),
                  pl.BlockSpec((LANES, LANES), full)],
        out_specs=[pl.BlockSpec((tm, w), row), pl.BlockSpec((tm, 2 * w), row)],
        out_shape=[jax.ShapeDtypeStruct((t, w), BF16), jax.ShapeDtypeStruct((t, 2 * w), F32)],
        compiler_params=_params(("parallel",)),
        name="qk_prep",
    )(proj, proj, proj, gq, gk, cos_t, sin_lo_t, sin_hi_t, ones_blk)


def _subln(o, g_ref, out_scale):
    return o * lax.rsqrt(jnp.mean(o * o, axis=-1, keepdims=True) + EPS) * g_ref[...] * out_scale


def _online_softmax_step(s, v, m_ref, l_ref, acc_ref, idx):
    m_prev = m_ref[idx]
    m_new = jnp.maximum(m_prev, jnp.max(s, axis=-1, keepdims=True))
    alpha = jnp.exp(m_prev - m_new)
    p = jnp.exp(s - m_new)
    l_ref[idx] = alpha * l_ref[idx] + jnp.sum(p, axis=-1, keepdims=True)
    acc_ref[idx] = alpha * acc_ref[idx] + jnp.dot(p.astype(BF16), v, preferred_element_type=F32)
    m_ref[idx] = m_new


def _flash_kernel(lam_ref, q_ref, k_ref, v_ref, g_ref, o_ref, m_ref, l_ref, acc_ref, *, tq, hd, out_scale):
    qi, ki = pl.program_id(2), pl.program_id(3)

    @pl.when(ki == 0)
    def _():
        m_ref[...] = jnp.full(m_ref.shape, NEG_INF, F32)
        l_ref[...] = jnp.zeros(l_ref.shape, F32)
        acc_ref[...] = jnp.zeros(acc_ref.shape, F32)

    def step(masked):
        q = q_ref[0]
        k = k_ref[0].astype(BF16)
        v = v_ref[0].astype(BF16)
        lane = lax.broadcasted_iota(jnp.int32, q.shape, 1)
        for s in range(2):
            qs = jnp.where((lane >= s * hd) == (lane < (s + 1) * hd), q, jnp.zeros_like(q))
            sc = lax.dot_general(qs, k, (((1,), (1,)), ((), ())), preferred_element_type=F32)
            if masked:
                row = lax.broadcasted_iota(jnp.int32, sc.shape, 0)
                col = lax.broadcasted_iota(jnp.int32, sc.shape, 1)
                sc = jnp.where(col <= row, sc, NEG_INF)
            _online_softmax_step(sc, v, m_ref, l_ref, acc_ref, s)

    @pl.when(ki < qi)
    def _():
        step(False)

    @pl.when(ki == qi)
    def _():
        step(True)
        o = acc_ref[0] / l_ref[0] - lam_ref[...] * (acc_ref[1] / l_ref[1])
        o_ref[0] = _subln(o, g_ref, out_scale).astype(o_ref.dtype)


def _flash_attention(q3, kv3, lam, subln_g, *, nh, hd, out_scale, tq=1024):
    b, l, w = q3.shape
    tq = min(tq, l)
    assert l % tq == 0
    nq = l // tq
    return pl.pallas_call(
        functools.partial(_flash_kernel, tq=tq, hd=hd, out_scale=out_scale),
        grid=(b, nh, nq, nq),
        in_specs=[pl.BlockSpec((1, 1), lambda bb, h, qi, ki: (0, 0)),
                  pl.BlockSpec((1, tq, LANES), lambda bb, h, qi, ki: (bb, qi, h)),
                  pl.BlockSpec((1, tq, LANES), lambda bb, h, qi, ki: (bb, jnp.minimum(ki, qi), h)),
                  pl.BlockSpec((1, tq, LANES), lambda bb, h, qi, ki: (bb, jnp.minimum(ki, qi), nh + h)),
                  pl.BlockSpec((1, LANES), lambda bb, h, qi, ki: (0, 0))],
        out_specs=pl.BlockSpec((1, tq, LANES), lambda bb, h, qi, ki: (bb, qi, h)),
        out_shape=jax.ShapeDtypeStruct((b, l, w), BF16),
        scratch_shapes=[pltpu.VMEM((2, tq, 1), F32), pltpu.VMEM((2, tq, 1), F32), pltpu.VMEM((2, tq, LANES), F32)],
        compiler_params=_params(("parallel", "parallel", "parallel", "arbitrary")),
        name="diff_attn_prompt",
    )(lam, q3, kv3, kv3, subln_g)


def _decode_kernel(pt_ref, lam_ref, q_ref, kvn_ref, *rest, npg, nh, hd, dl, psz, out_scale):
    page_refs = rest[:npg]
    g_ref, o_ref, qw_s, s_s, v_s, m_s = rest[npg:]
    p = pl.program_id(1)
    cols = nh * 2 * dl
    nt = (((1,), (1,)), ((), ()))

    @pl.when(p == 0)
    def _():
        q = q_ref[0].astype(F32)
        lane = lax.broadcasted_iota(jnp.int32, (dl, LANES), 1)
        pieces = []
        for h in range(nh):
            qh = q[:, h * LANES:(h + 1) * LANES]
            for s in range(2):
                pieces.append(jnp.where((lane >= s * hd) == (lane < (s + 1) * hd), qh, 0.0))
        qw_s[...] = jnp.concatenate(pieces, axis=0).astype(BF16)
        m_s[...] = jnp.full(m_s.shape, NEG_INF, F32)

    for i, ref in enumerate(page_refs):
        pg = ref[...].reshape(psz, 2 * nh, LANES)
        k2 = pg[:, :nh, :].reshape(psz * nh, LANES).astype(BF16)
        v2 = pg[:, nh:, :].reshape(psz * nh, LANES).astype(BF16)
        sc = lax.dot_general(k2, qw_s[...], nt, preferred_element_type=F32)
        slot = p * npg + i
        s_s[slot] = sc
        v_s[slot] = v2
        m_s[...] = jnp.maximum(m_s[...], jnp.max(sc.reshape(psz, nh, cols), axis=0))

    @pl.when(p == pl.num_programs(1) - 1)
    def _():
        w = nh * LANES
        kvn = kvn_ref[0]
        pad = jnp.zeros((LANES - nh * dl, LANES), F32)
        kn = jnp.concatenate([kvn[:, h * LANES:(h + 1) * LANES] for h in range(nh)] + [pad], axis=0)
        vn = jnp.concatenate([kvn[:, w + h * LANES:w + (h + 1) * LANES] for h in range(nh)] + [pad], axis=0)
        sn = lax.dot_general(kn.astype(BF16), qw_s[...], nt, preferred_element_type=F32)
        r = lax.broadcasted_iota(jnp.int32, sn.shape, 0)
        c = lax.broadcasted_iota(jnp.int32, sn.shape, 1)
        same_head = r // dl == c // (2 * dl)
        causal = r % dl <= c % dl
        sn_m = jnp.where(same_head, jnp.where(causal, sn, NEG_INF), NEG_INF)
        hr = lax.broadcasted_iota(jnp.int32, (nh, cols), 0)
        hc = lax.broadcasted_iota(jnp.int32, (nh, cols), 1)
        diag = hr == hc // (2 * dl)
        m_col = jnp.maximum(jnp.max(sn_m, axis=0, keepdims=True),
                            jnp.max(jnp.where(diag, m_s[...], NEG_INF), axis=0, keepdims=True))
        pn = jnp.where(same_head, jnp.where(causal, jnp.exp(sn - m_col), 0.0), 0.0)
        l0 = jnp.sum(pn, axis=0, keepdims=True)
        o0 = jnp.dot(pn.T.astype(BF16), vn.astype(BF16), preferred_element_type=F32)

        def page_body(slot, carry):
            l_acc, o_acc = carry
            s3 = s_s[slot].reshape(psz, nh, cols)
            pm = jnp.where(diag[None], jnp.exp(s3 - m_col[None]), 0.0)
            l_acc = l_acc + jnp.sum(jnp.sum(pm, axis=0), axis=0, keepdims=True)
            pt = pm.reshape(psz * nh, cols).T.astype(BF16)
            return l_acc, o_acc + jnp.dot(pt, v_s[slot], preferred_element_type=F32)

        l_row, o = lax.fori_loop(0, s_s.shape[0], page_body, (l0, o0))
        er = lax.broadcasted_iota(jnp.int32, (cols, cols), 0)
        ec = lax.broadcasted_iota(jnp.int32, (cols, cols), 1)
        l_col = jnp.sum(jnp.where(er == ec, jnp.broadcast_to(l_row, (cols, cols)), 0.0), axis=1, keepdims=True)
        o = o / l_col
        for h in range(nh):
            r1 = slice(h * 2 * dl, h * 2 * dl + dl)
            r2 = slice(h * 2 * dl + dl, (h + 1) * 2 * dl)
            oo = o[r1] - lam_ref[...] * o[r2]
            o_ref[0, :, h * LANES:(h + 1) * LANES] = _subln(oo, g_ref, out_scale).astype(o_ref.dtype)


def _decode_attention(q3, kvn3, cache, layer, page_table, lam, subln_g, *, nh, hd, out_scale, npg=8):
    db, dl, w = q3.shape
    n_pages = page_table.shape[1]
    prow = cache.shape[2]
    psz = prow // (2 * nh)
    cols = nh * 2 * dl
    assert n_pages % npg == 0 and dl == SUBLANES and cols == LANES and 2 * hd == LANES

    def page_spec(i):
        return pl.BlockSpec((pl.Squeezed(), pl.Squeezed(), prow, LANES),
                            lambda b, p, pt: (layer, pt[b, p * npg + i], 0, 0))

    grid_spec = pltpu.PrefetchScalarGridSpec(
        num_scalar_prefetch=1,
        grid=(db, n_pages // npg),
        in_specs=[pl.BlockSpec((1, 1), lambda b, p, pt: (0, 0)),
                  pl.BlockSpec((1, dl, w), lambda b, p, pt: (b, 0, 0)),
                  pl.BlockSpec((1, dl, 2 * w), lambda b, p, pt: (b, 0, 0))]
        + [page_spec(i) for i in range(npg)]
        + [pl.BlockSpec((1, LANES), lambda b, p, pt: (0, 0))],
        out_specs=pl.BlockSpec((1, dl, w), lambda b, p, pt: (b, 0, 0)),
        scratch_shapes=[pltpu.VMEM((cols, LANES), BF16),
                        pltpu.VMEM((n_pages, psz * nh, cols), F32),
                        pltpu.VMEM((n_pages, psz * nh, LANES), BF16),
                        pltpu.VMEM((nh, cols), F32)],
    )
    return pl.pallas_call(
        functools.partial(_decode_kernel, npg=npg, nh=nh, hd=hd, dl=dl, psz=psz, out_scale=out_scale),
        grid_spec=grid_spec,
        out_shape=jax.ShapeDtypeStruct((db, dl, w), BF16),
        compiler_params=_params(("parallel", "arbitrary")),
        name="diff_attn_sample",
    )(page_table, lam, q3, kvn3, *([cache] * npg), subln_g)


def _s5_kernel(*refs, n_u, nseq, ts, lc, npk, out3d):
    u_refs = refs[:n_u]
    (wb_ref, wc_ref, d_ref, ar_ref, ai_ref, h0r_ref, h0i_ref, wg_ref, bg_ref,
     y_ref, hr_out, hi_out, xr_s, xi_s, us_s, ys_s, hst_s) = refs[n_u:]
    s = pl.program_id(1)
    nlb = xr_s.shape[0]
    nst = nlb * LANES
    pb = nlb // npk
    pw = pb * LANES
    cw = d_ref.shape[1] // npk
    gb = lc // LANES

    @pl.when(s == 0)
    def _():
        hst_s[0] = h0r_ref[0]
        hst_s[1] = h0i_ref[0]

    u = jnp.concatenate([r[...] for r in u_refs], axis=0) if n_u > 1 else u_refs[0][...]
    for q in range(nseq):
        for b in range(us_s.shape[0]):
            us_s[b, pl.ds(q, ts, stride=nseq), :] = u[q * ts:(q + 1) * ts, b * LANES:(b + 1) * LANES]
    ub = jnp.concatenate([us_s[b] for b in range(us_s.shape[0])], axis=1).astype(BF16)
    for pk in range(npk):
        x = jnp.dot(ub[:, pk * cw:(pk + 1) * cw], wb_ref[pk], preferred_element_type=F32)
        for q in range(pb):
            xr_s[pk * pb + q] = x[:, q * LANES:(q + 1) * LANES]
            xi_s[pk * pb + q] = x[:, pw + q * LANES:pw + (q + 1) * LANES]

    for c in range(nlb // gb):
        blks = list(range(c * gb, (c + 1) * gb))
        ar = [jnp.broadcast_to(ar_ref[:, q * LANES:(q + 1) * LANES], (nseq, LANES)) for q in blks]
        ai = [jnp.broadcast_to(ai_ref[:, q * LANES:(q + 1) * LANES], (nseq, LANES)) for q in blks]

        def body(t, carry, blks=blks, ar=ar, ai=ai):
            idx = pl.ds(pl.multiple_of(t * nseq, nseq), nseq)
            out = []
            for n, q in enumerate(blks):
                hr, hi = carry[2 * n], carry[2 * n + 1]
                nr = ar[n] * hr - ai[n] * hi + xr_s[q, idx, :]
                ni = ar[n] * hi + ai[n] * hr + xi_s[q, idx, :]
                xr_s[q, idx, :] = nr
                xi_s[q, idx, :] = ni
                out += [nr, ni]
            return tuple(out)

        init = []
        for q in blks:
            init += [hst_s[0, :, q * LANES:(q + 1) * LANES], hst_s[1, :, q * LANES:(q + 1) * LANES]]
        fin = lax.fori_loop(0, ts, body, tuple(init), unroll=2)
        for n, q in enumerate(blks):
            hst_s[0, :, q * LANES:(q + 1) * LANES] = fin[2 * n]
            hst_s[1, :, q * LANES:(q + 1) * LANES] = fin[2 * n + 1]

    for pk in range(npk):
        hr_b = jnp.concatenate([xr_s[pk * pb + q] for q in range(pb)], axis=1).astype(BF16)
        hi_b = jnp.concatenate([xi_s[pk * pb + q] for q in range(pb)], axis=1).astype(BF16)
        ys_s[pk] = (jnp.dot(hr_b, wc_ref[0, pk], preferred_element_type=F32)
                    + jnp.dot(hi_b, wc_ref[1, pk], preferred_element_type=F32))
    y_state = jnp.concatenate(
        [jnp.concatenate([ys_s[pk, pl.ds(q, ts, stride=nseq), :] for pk in range(npk)], axis=1)
         for q in range(nseq)], axis=0)
    y = y_state + d_ref[...] * u
    g = _gelu(y)
    out = g * _sigmoid(jnp.dot(g.astype(BF16), wg_ref[...], preferred_element_type=F32) + bg_ref[...])
    if out3d:
        y_ref[...] = out.reshape(y_ref.shape).astype(y_ref.dtype)
    else:
        y_ref[...] = out.astype(y_ref.dtype)

    @pl.when(s == pl.num_programs(1) - 1)
    def _():
        hr_out[0] = hst_s[0]
        hi_out[0] = hst_s[1]


def _s5(proj, h0r, h0i, wts, *, row0, nchunk, nseq, slen, ts, prompt):
    wb, wc, dd, ar, ai, wg, bg = wts
    npk = wb.shape[0]
    ch = dd.shape[1]
    nst = ar.shape[1]
    nsteps = slen // ts
    rows = nseq * ts
    lc = max(LANES, 1024 // (-(-nseq // SUBLANES)))
    if prompt:
        assert nchunk == 1 and row0 == 0
        u_specs = [pl.BlockSpec((ts, ch), lambda c, s, q=q: (q * nsteps + s, 0)) for q in range(nseq)]
        y_spec = pl.BlockSpec((nseq, ts, ch), lambda c, s: (0, s, 0))
        y_shape = jax.ShapeDtypeStruct((nseq, slen, ch), BF16)
    else:
        assert nsteps == 1 and row0 % rows == 0
        u_specs = [pl.BlockSpec((rows, ch), lambda c, s: (row0 // rows + c, 0))]
        y_spec = pl.BlockSpec((rows, ch), lambda c, s: (c, 0))
        y_shape = jax.ShapeDtypeStruct((nchunk * rows, ch), BF16)
    n_u = len(u_specs)
    full2 = lambda c, s: (0, 0)
    st_spec = pl.BlockSpec((1, nseq, nst), lambda c, s: (c, 0, 0))
    return pl.pallas_call(
        functools.partial(_s5_kernel, n_u=n_u, nseq=nseq, ts=ts, lc=lc, npk=npk, out3d=prompt),
        grid=(nchunk, nsteps),
        in_specs=u_specs + [
            pl.BlockSpec(wb.shape, lambda c, s: (0, 0, 0)),
            pl.BlockSpec(wc.shape, lambda c, s: (0, 0, 0, 0)),
            pl.BlockSpec((1, ch), full2), pl.BlockSpec((1, nst), full2), pl.BlockSpec((1, nst), full2),
            st_spec, st_spec,
            pl.BlockSpec(wg.shape, full2), pl.BlockSpec((1, ch), full2)],
        out_specs=[y_spec, st_spec, st_spec],
        out_shape=[y_shape, jax.ShapeDtypeStruct((nchunk, nseq, nst), F32),
                   jax.ShapeDtypeStruct((nchunk, nseq, nst), F32)],
        scratch_shapes=[pltpu.VMEM((nst // LANES, rows, LANES), F32), pltpu.VMEM((nst // LANES, rows, LANES), F32),
                        pltpu.VMEM((ch // LANES, rows, LANES), F32),
                        pltpu.VMEM((npk, rows, ch // npk), F32),
                        pltpu.VMEM((2, nseq, nst), F32)],
        compiler_params=_params(("parallel", "arbitrary")),
        name="s5_prompt" if prompt else "s5_sample",
    )(*([proj] * n_u), wb, wc, dd, ar, ai, h0r, h0i, wg, bg)


def _lru_kernel(*refs, n_x, nseq, ts, lc, nblk, out3d):
    g_refs = refs[:n_x]
    x_refs = refs[n_x:2 * n_x]
    (cw_ref, cb_ref, wa_ref, ba_ref, wx_ref, bx_ref, sp_ref, buf_ref, h0_ref,
     y_ref, buf_out, h_out, a_s, b_s, prev_s, h_s) = refs[2 * n_x:]
    s = pl.program_id(1)
    dr = cb_ref.shape[1]
    bs = dr // nblk
    rows = nseq * ts

    @pl.when(s == 0)
    def _():
        prev_s[...] = buf_ref[0]
        h_s[...] = h0_ref[0]

    if n_x > 1:
        x3 = jnp.stack([r[...] for r in x_refs], axis=0)
        gate = jnp.concatenate([r[...] for r in g_refs], axis=0)
    else:
        x3 = x_refs[0][...].reshape(nseq, ts, dr)
        gate = g_refs[0][...]
    xx = jnp.concatenate([prev_s[...], x3], axis=1)
    kw = cw_ref.shape[0]
    conv = cb_ref[...] + xx[:, SUBLANES - (kw - 1):SUBLANES - (kw - 1) + ts] * cw_ref[0:1, :]
    for tap in range(1, kw):
        off = SUBLANES - (kw - 1) + tap
        conv = conv + xx[:, off:off + ts] * cw_ref[tap:tap + 1, :]
    prev_s[...] = xx[:, ts:ts + SUBLANES]

    c2 = conv.reshape(rows, dr)
    cb16 = c2.astype(BF16)
    rs, is_ = [], []
    for n in range(nblk):
        blk = cb16[:, n * bs:(n + 1) * bs]
        rs.append(jnp.dot(blk, wa_ref[n], preferred_element_type=F32))
        is_.append(jnp.dot(blk, wx_ref[n], preferred_element_type=F32))
    r = _sigmoid(jnp.concatenate(rs, axis=1) + ba_ref[...])
    i = _sigmoid(jnp.concatenate(is_, axis=1) + bx_ref[...])
    log_a = -LRU_C * r * sp_ref[...]
    a_all = jnp.exp(log_a)
    b_all = jnp.sqrt(1.0 - jnp.exp(2.0 * log_a)) * (i * c2)
    nlb = dr // LANES
    for q in range(nlb):
        a_s[q] = a_all[:, q * LANES:(q + 1) * LANES]
        b_s[q] = b_all[:, q * LANES:(q + 1) * LANES]

    gb = lc // LANES
    for c in range(nlb // gb):
        blks = list(range(c * gb, (c + 1) * gb))

        def body(t, carry, blks=blks):
            idx = pl.ds(t, nseq, stride=ts)
            out = []
            for n, q in enumerate(blks):
                h = a_s[q, idx, :] * carry[n] + b_s[q, idx, :]
                b_s[q, idx, :] = h
                out.append(h)
            return tuple(out)

        fin = lax.fori_loop(0, ts, body, tuple(h_s[:, q * LANES:(q + 1) * LANES] for q in blks))
        for n, q in enumerate(blks):
            h_s[:, q * LANES:(q + 1) * LANES] = fin[n]

    out = _gelu(gate) * jnp.concatenate([b_s[q] for q in range(nlb)], axis=1)
    if out3d:
        y_ref[...] = out.reshape(y_ref.shape).astype(y_ref.dtype)
    else:
        y_ref[...] = out.astype(y_ref.dtype)

    @pl.when(s == pl.num_programs(1) - 1)
    def _():
        buf_out[0] = prev_s[...]
        h_out[0] = h_s[...]


def _lru(proj, buf8, h0, wts, *, row0, nchunk, nseq, slen, ts, prompt):
    cw, cb, wa, ba, wx, bx, sp = wts
    dr = cb.shape[1]
    nblk = wa.shape[0]
    nsteps = slen // ts
    rows = nseq * ts
    lc = dr if nseq <= SUBLANES else 256
    assert dr % lc == 0
    if prompt:
        assert nchunk == 1 and row0 == 0
        g_specs = [pl.BlockSpec((ts, dr), lambda c, s, q=q: (q * nsteps + s, 0)) for q in range(nseq)]
        x_specs = [pl.BlockSpec((ts, dr), lambda c, s, q=q: (q * nsteps + s, 1)) for q in range(nseq)]
        y_spec = pl.BlockSpec((nseq, ts, dr), lambda c, s: (0, s, 0))
        y_shape = jax.ShapeDtypeStruct((nseq, slen, dr), BF16)
    else:
        assert nsteps == 1 and row0 % rows == 0 and ts == SUBLANES
        g_specs = [pl.BlockSpec((rows, dr), lambda c, s: (row0 // rows + c, 0))]
        x_specs = [pl.BlockSpec((rows, dr), lambda c, s: (row0 // rows + c, 1))]
        y_spec = pl.BlockSpec((rows, dr), lambda c, s: (c, 0))
        y_shape = jax.ShapeDtypeStruct((nchunk * rows, dr), BF16)
    n_x = len(x_specs)
    full2 = lambda c, s: (0, 0)
    full3 = lambda c, s: (0, 0, 0)
    return pl.pallas_call(
        functools.partial(_lru_kernel, n_x=n_x, nseq=nseq, ts=ts, lc=lc, nblk=nblk, out3d=prompt),
        grid=(nchunk, nsteps),
        in_specs=g_specs + x_specs + [
            pl.BlockSpec(cw.shape, full2), pl.BlockSpec((1, dr), full2),
            pl.BlockSpec(wa.shape, full3), pl.BlockSpec((1, dr), full2),
            pl.BlockSpec(wx.shape, full3), pl.BlockSpec((1, dr), full2),
            pl.BlockSpec((1, dr), full2),
            pl.BlockSpec((1, nseq, SUBLANES, dr), lambda c, s: (c, 0, 0, 0)),
            pl.BlockSpec((1, nseq, dr), lambda c, s: (c, 0, 0))],
        out_specs=[y_spec,
                   pl.BlockSpec((1, nseq, SUBLANES, dr), lambda c, s: (c, 0, 0, 0)),
                   pl.BlockSpec((1, nseq, dr), lambda c, s: (c, 0, 0))],
        out_shape=[y_shape, jax.ShapeDtypeStruct((nchunk, nseq, SUBLANES, dr), F32),
                   jax.ShapeDtypeStruct((nchunk, nseq, dr), F32)],
        scratch_shapes=[pltpu.VMEM((dr // LANES, rows, LANES), F32), pltpu.VMEM((dr // LANES, rows, LANES), F32),
                        pltpu.VMEM((nseq, SUBLANES, dr), F32), pltpu.VMEM((nseq, dr), F32)],
        compiler_params=_params(("parallel", "arbitrary")),
        name="rglru_prompt" if prompt else "rglru_sample",
    )(*([proj] * (2 * n_x)), cw, cb, wa, ba, wx, bx, sp, buf8, h0)


def _peer_scores_kernel(x_ref, g_ref, sc_ref, sh_ref, wq_ref, k1_ref, k2_ref, hf_ref, s1_ref, s2_ref, *,
                        tm, nh, dk):
    y = _norm_mod(x_ref[...], g_ref, sc_ref, sh_ref)
    hf = y.reshape(tm, y.shape[-1]).astype(BF16)
    hf_ref[...] = hf
    q = jnp.dot(hf, wq_ref[...], preferred_element_type=F32)
    half = dk // 2
    q1, q2 = [], []
    for h in range(nh):
        qh = q[:, h * dk:(h + 1) * dk]
        qn = qh * lax.rsqrt(jnp.mean(qh * qh, axis=-1, keepdims=True) + EPS)
        q1.append(qn[:, :half])
        q2.append(qn[:, half:])
    nt = (((1,), (1,)), ((), ()))
    s1_ref[...] = lax.dot_general(k1_ref[...], jnp.concatenate(q1, axis=1).astype(BF16), nt,
                                  preferred_element_type=F32)
    s2_ref[...] = lax.dot_general(k2_ref[...], jnp.concatenate(q2, axis=1).astype(BF16), nt,
                                  preferred_element_type=F32)


def _peer_scores(x3, norm, wq, kexp1, kexp2, *, nh, tm=512):
    t8, _, d = x3.shape
    t = t8 * SUBLANES
    assert t % tm == 0
    g8 = tm // SUBLANES
    nq = wq.shape[1]
    nrow = kexp1.shape[0]
    full2 = lambda i: (0, 0)
    return pl.pallas_call(
        functools.partial(_peer_scores_kernel, tm=tm, nh=nh, dk=nq // nh),
        grid=(t // tm,),
        in_specs=[pl.BlockSpec((g8, SUBLANES, d), lambda i: (i, 0, 0)),
                  pl.BlockSpec((1, 1, d), lambda i: (0, 0, 0)),
                  pl.BlockSpec((g8, 1, d), lambda i: (i, 0, 0)),
                  pl.BlockSpec((g8, 1, d), lambda i: (i, 0, 0)),
                  pl.BlockSpec(wq.shape, full2), pl.BlockSpec(kexp1.shape, full2), pl.BlockSpec(kexp2.shape, full2)],
        out_specs=[pl.BlockSpec((tm, d), lambda i: (i, 0)),
                   pl.BlockSpec((nrow, tm), lambda i: (0, i)),
                   pl.BlockSpec((nrow, tm), lambda i: (0, i))],
        out_shape=[jax.ShapeDtypeStruct((t, d), BF16), jax.ShapeDtypeStruct((nrow, t), F32),
                   jax.ShapeDtypeStruct((nrow, t), F32)],
        compiler_params=_params(("parallel",)),
        name="peer_scores",
    )(x3, *norm, wq, kexp1, kexp2)


def _hyperbola(k):
    return [(a, b) for a in range(k) for b in range(k) if (a + 1) * (b + 1) <= k]


def _peer_topk_kernel(s1_ref, s2_ref, e1_ref, e2_ref, g_ref, x_s, v_s, i_s, *, nkeys, nh, topk):
    tl = s1_ref.shape[1]
    kid = lax.broadcasted_iota(jnp.int32, (nkeys, nh, tl), 0)

    for side, s_ref in enumerate((s1_ref, s2_ref)):
        x_s[...] = s_ref[...].reshape(nkeys, nh, tl)

        def body(it, m, side=side):
            x = x_s[...]
            idx = jnp.min(jnp.where(x == m[None], kid, nkeys), axis=0)
            v_s[side, it] = m
            i_s[side, it] = idx
            xn = jnp.where(kid == idx[None], NEG_INF, x)
            x_s[...] = xn
            return jnp.max(xn, axis=0)

        lax.fori_loop(0, topk, body, jnp.max(x_s[...], axis=0))

    pairs = _hyperbola(topk)
    v1 = [v_s[0, k] for k in range(topk)]
    v2 = [v_s[1, k] for k in range(topk)]
    cand = [v1[a] + v2[b] for a, b in pairs]
    flat = [a * topk + b for a, b in pairs]
    big = topk * topk
    shift = topk.bit_length() - 1
    assert 1 << shift == topk
    tops, k1s, k2s = [], [], []
    for _ in range(topk):
        m = functools.reduce(jnp.maximum, cand)
        sel = functools.reduce(jnp.minimum, [jnp.where(c == m, f, big) for c, f in zip(cand, flat)])
        cand = [jnp.where(sel == f, NEG_INF, c) for c, f in zip(cand, flat)]
        tops.append(m)
        k1s.append(lax.shift_right_logical(sel, shift))
        k2s.append(sel & (topk - 1))
    i1 = [i_s[0, k] for k in range(topk)]
    i2 = [i_s[1, k] for k in range(topk)]
    ex = [jnp.exp(tv - tops[0]) for tv in tops]
    den = functools.reduce(jnp.add, ex)
    e1, e2, gs = [], [], []
    for k in range(topk):
        e1.append(functools.reduce(jnp.add, [jnp.where(k1s[k] == j, i1[j], 0) for j in range(topk)]))
        e2.append(functools.reduce(jnp.add, [jnp.where(k2s[k] == j, i2[j], 0) for j in range(topk)]))
        gs.append(ex[k] / den)
    e1_ref[...] = jnp.concatenate(e1, axis=0).astype(F32).T
    e2_ref[...] = jnp.concatenate(e2, axis=0).astype(F32).T
    g_ref[...] = jnp.concatenate(gs, axis=0).T


def _peer_topk(s1, s2, *, nkeys, nh, topk=PEER_TOPK, tl=LANES):
    nrow, t = s1.shape
    assert nrow == nkeys * nh and t % tl == 0 and nh * topk == LANES
    in_spec = pl.BlockSpec((nrow, tl), lambda i: (0, i))
    out_spec = pl.BlockSpec((tl, nh * topk), lambda i: (i, 0))
    out = jax.ShapeDtypeStruct((t, nh * topk), F32)
    return pl.pallas_call(
        functools.partial(_peer_topk_kernel, nkeys=nkeys, nh=nh, topk=topk),
        grid=(t // tl,),
        in_specs=[in_spec, in_spec],
        out_specs=[out_spec, out_spec, out_spec],
        out_shape=[out, out, out],
        scratch_shapes=[pltpu.VMEM((nkeys, nh, tl), F32), pltpu.VMEM((2, topk, nh, tl), F32),
                        pltpu.VMEM((2, topk, nh, tl), jnp.int32)],
        compiler_params=_params(("parallel",)),
        name="peer_topk",
    )(s1, s2)


def _peer_dense_kernel(hf_ref, e1_ref, e2_ref, g_ref, u_ref, v_ref, r_ref, gt_ref, o_ref, g3_s, acc_s, w_s, *,
                       tb, nkeys, eb1):
    j = pl.program_id(1)

    @pl.when(j == 0)
    def _():
        acc_s[...] = jnp.zeros(acc_s.shape, F32)
        kio = lax.broadcasted_iota(jnp.int32, (nkeys, LANES), 0).astype(F32)
        nt = (((1,), (1,)), ((), ()))

        def build(t, carry):
            e1 = jnp.broadcast_to(e1_ref[pl.ds(t, 1), :], (nkeys, LANES))
            e2 = jnp.broadcast_to(e2_ref[pl.ds(t, 1), :], (nkeys, LANES))
            gg = jnp.broadcast_to(g_ref[pl.ds(t, 1), :], (nkeys, LANES))
            a = jnp.where(kio == e1, gg, 0.0)
            a_hi = a.astype(BF16)
            a_lo = (a - a_hi.astype(F32)).astype(BF16)
            b = jnp.where(kio == e2, 1.0, 0.0).astype(BF16)
            gmat = lax.dot_general(jnp.concatenate([a_hi, a_lo], axis=1), jnp.concatenate([b, b], axis=1), nt,
                                   preferred_element_type=F32)
            g3_s[pl.ds(pl.multiple_of(t * nkeys, nkeys), nkeys), :] = gmat
            return carry

        lax.fori_loop(0, tb, build, 0, unroll=8)

    x = hf_ref[...]
    cw = 2 * nkeys
    for c in range(eb1 // 2):
        z = lax.dot_general(x, u_ref[c * cw:(c + 1) * cw, :], (((1,), (1,)), ((), ())),
                            preferred_element_type=F32)
        gate = jnp.concatenate(
            [g3_s[pl.ds(j * eb1 + 2 * c + a, tb, stride=nkeys), :] for a in range(2)], axis=1)
        w_s[:, c * cw:(c + 1) * cw] = (gate * _gelu(z)).astype(BF16)
    acc_s[...] += jnp.dot(w_s[...], v_ref[...], preferred_element_type=F32)

    @pl.when(j == pl.num_programs(1) - 1)
    def _():
        o_ref[...] = r_ref[...] + gt_ref[...] * acc_s[...].reshape(o_ref.shape)


def _peer_dense(hf, e1, e2, g, u_tab, v_tab, resid, gate, *, nkeys, tb=256, eb1=16):
    t, d = hf.shape
    ne = u_tab.shape[0]
    assert ne == nkeys * nkeys and t % tb == 0 and nkeys % eb1 == 0 and eb1 % 2 == 0
    eb = eb1 * nkeys
    g8 = tb // SUBLANES
    return pl.pallas_call(
        functools.partial(_peer_dense_kernel, tb=tb, nkeys=nkeys, eb1=eb1),
        grid=(t // tb, ne // eb),
        in_specs=[pl.BlockSpec((tb, d), lambda i, j: (i, 0)),
                  pl.BlockSpec((tb, LANES), lambda i, j: (i, 0)),
                  pl.BlockSpec((tb, LANES), lambda i, j: (i, 0)),
                  pl.BlockSpec((tb, LANES), lambda i, j: (i, 0)),
                  pl.BlockSpec((eb, d), lambda i, j: (j, 0)),
                  pl.BlockSpec((eb, d), lambda i, j: (j, 0)),
                  pl.BlockSpec((g8, SUBLANES, d), lambda i, j: (i, 0, 0)),
                  pl.BlockSpec((g8, 1, d), lambda i, j: (i, 0, 0))],
        out_specs=pl.BlockSpec((g8, SUBLANES, d), lambda i, j: (i, 0, 0)),
        out_shape=jax.ShapeDtypeStruct((t // SUBLANES, SUBLANES, d), F32),
        scratch_shapes=[pltpu.VMEM((tb * nkeys, LANES), F32), pltpu.VMEM((tb, d), F32),
                        pltpu.VMEM((tb, eb), BF16)],
        compiler_params=_params(("parallel", "arbitrary"), vmem_mb=56),
        name="peer_dense",
    )(hf, e1, e2, g, u_tab, v_tab, resid, gate)


def _s5_weights(lam_re, lam_im, b_re, b_im, c_re, c_im, d, log_dt, w_glu, b_glu, *, gpp=8):
    ng, ns = lam_re.shape
    p = d.shape[1]
    dt = jnp.exp(log_dt)[:, None]
    mag = jnp.exp(lam_re * dt)
    ab_re, ab_im = mag * jnp.cos(lam_im * dt), mag * jnp.sin(lam_im * dt)
    den = lam_re * lam_re + lam_im * lam_im
    fr = ((ab_re - 1.0) * lam_re + ab_im * lam_im) / den
    fi = (ab_im * lam_re - (ab_re - 1.0) * lam_im) / den
    xb_re = fr[..., None] * b_re - fi[..., None] * b_im
    xb_im = fr[..., None] * b_im + fi[..., None] * b_re
    npk = ng // gpp
    eye = jnp.eye(gpp, dtype=F32)

    def pack_b(w):
        w = w.reshape(npk, gpp, ns, p)
        return jnp.einsum("kgnp,gh->kgphn", w, eye).reshape(npk, gpp * p, gpp * ns)

    def pack_c(w):
        w = w.reshape(npk, gpp, p, ns)
        return jnp.einsum("kgpn,gh->kgnhp", w, eye).reshape(npk, gpp * ns, gpp * p)

    wb = jnp.concatenate([pack_b(xb_re), pack_b(xb_im)], axis=2).astype(BF16)
    wc = jnp.stack([pack_c(c_re), -pack_c(c_im)], axis=0).astype(BF16)
    return (wb, wc, d.reshape(1, ng * p), ab_re.reshape(1, ng * ns), ab_im.reshape(1, ng * ns),
            w_glu.astype(BF16), b_glu.reshape(1, -1))


def _rope_tables(pos, hd):
    rot = hd // 4
    inv = jnp.power(ROPE_THETA, -jnp.arange(0, rot, 2, dtype=F32) / rot)
    ang = pos.astype(F32)[:, None] * inv[None, :]
    cos, sin = jnp.cos(ang), jnp.sin(ang)
    n = pos.shape[0]
    half = rot // 2
    ones = jnp.ones((n, hd - rot), F32)
    zeros = jnp.zeros((n, hd - rot), F32)
    zh = jnp.zeros((n, half), F32)
    cos_t = jnp.concatenate([cos, cos, ones], axis=1)
    sin_lo = jnp.concatenate([-sin, zh, zeros], axis=1)
    sin_hi = jnp.concatenate([zh, sin, zeros], axis=1)
    rep = LANES // hd
    return jnp.tile(cos_t, (1, rep)), jnp.tile(sin_lo, (1, rep)), jnp.tile(sin_hi, (1, rep))


def kernel(x_prompt, x_sample, cache_kv, page_table, state_s5_re, state_s5_im, state_lru, state_conv, c_prompt, c_sample, norm_mix, norm_ffn, w_ada, b_ada, w_in_even, w_out_even, q_norm, k_norm, lambda_q1, lambda_k1, lambda_q2, lambda_k2, attn_subln, s5_lambda_re, s5_lambda_im, s5_b_re, s5_b_im, s5_c_re, s5_c_im, s5_d, s5_log_dt, s5_w_glu, s5_b_glu, w_in_odd, conv_w, conv_b, lru_w_a, lru_b_a, lru_w_x, lru_b_x, lru_lambda, w_out_odd, peer_wq, peer_keys, peer_u, peer_v):
    nb, sl, dm = x_prompt.shape
    db, dl, _ = x_sample.shape
    depth = norm_mix.shape[0]
    n_even = w_in_even.shape[0]
    nh = cache_kv.shape[4]
    hd = cache_kv.shape[5] // 2
    psz = cache_kv.shape[2]
    n_pool = cache_kv.shape[1]
    qkv_w = nh * 2 * hd
    s5_ch = w_in_even.shape[2] - 3 * qkv_w
    ng, ns = s5_lambda_re.shape[1], s5_lambda_re.shape[2]
    dr = conv_b.shape[1]
    kw = conv_w.shape[1]
    p_heads, _, nkeys, _ = peer_keys.shape[1:]
    tp, tsmp = nb * sl, db * dl
    t = tp + tsmp
    t8 = t // SUBLANES
    past_len = page_table.shape[1] * psz

    x = jnp.concatenate([x_prompt.reshape(tp, dm), x_sample.reshape(tsmp, dm)], axis=0).reshape(t8, SUBLANES, dm)
    c_all = jnp.concatenate([c_prompt, c_sample], axis=0)

    def expand(m):
        return jnp.concatenate([jnp.repeat(m[:nb], sl // SUBLANES, axis=0),
                                jnp.repeat(m[nb:], dl // SUBLANES, axis=0)], axis=0)[:, None, :]

    pos = jnp.concatenate([jnp.tile(jnp.arange(sl, dtype=jnp.int32), nb),
                           jnp.tile(past_len + jnp.arange(dl, dtype=jnp.int32), db)])
    cos_t, sin_lo, sin_hi = _rope_tables(pos, hd)
    lane = jnp.arange(LANES)
    ones_blk = (lane[:, None] // hd == lane[None, :] // hd).astype(BF16)
    eye_h = jnp.eye(p_heads, dtype=F32)
    cache_rows = cache_kv.reshape(n_even, n_pool, psz * 2 * nh, 2 * hd)
    rb = 32 if db % 32 == 0 else db
    ts_p = 64 if sl % 64 == 0 else sl

    kv_rows, s5_re, s5_im, lru_h, conv_bufs = [], [], [], [], []
    for layer in range(depth):
        j = layer // 2
        mod = _mod(c_all, w_ada[layer].astype(BF16), b_ada[layer])
        sh1, sc1, g1, sh2, sc2, g2 = [expand(m) for m in jnp.split(mod, 6, axis=-1)]
        nm = (norm_mix[layer].reshape(1, 1, dm), sc1, sh1)
        if layer % 2 == 0:
            proj = _linear([x], [w_in_even[j].astype(BF16)], norm=nm, name="in_proj_even").reshape(t, -1)
            tile2 = lambda g: jnp.tile(g, LANES // hd).reshape(1, LANES)
            qn, kv = _qkprep(proj, tile2(q_norm[j]), tile2(k_norm[j]), cos_t, sin_lo, sin_hi, ones_blk, nh=nh, hd=hd)
            kv_rows.append(kv)
            lam_init = 0.8 - 0.6 * math.exp(-0.3 * layer)
            lam = (jnp.exp(jnp.sum(lambda_q1[j] * lambda_k1[j])) - jnp.exp(jnp.sum(lambda_q2[j] * lambda_k2[j]))
                   + lam_init).reshape(1, 1)
            sub_g = attn_subln[j].reshape(1, LANES)
            o_p = _flash_attention(qn[:tp].reshape(nb, sl, qkv_w), kv[:tp].reshape(nb, sl, 2 * qkv_w), lam, sub_g,
                                   nh=nh, hd=hd, out_scale=1.0 - lam_init)
            o_s = _decode_attention(qn[tp:].reshape(db, dl, qkv_w), kv[tp:].reshape(db, dl, 2 * qkv_w), cache_rows,
                                    j, page_table, lam, sub_g, nh=nh, hd=hd, out_scale=1.0 - lam_init)
            wts = _s5_weights(s5_lambda_re[j], s5_lambda_im[j], s5_b_re[j], s5_b_im[j], s5_c_re[j], s5_c_im[j],
                              s5_d[j], s5_log_dt[j], s5_w_glu[j], s5_b_glu[j])
            z_s5 = jnp.zeros((1, nb, ng * ns), F32)
            y_p, hr_p, hi_p = _s5(proj, z_s5, z_s5, wts, row0=0, nchunk=1, nseq=nb, slen=sl, ts=ts_p, prompt=True)
            y_s, hr_s, hi_s = _s5(proj, state_s5_re[j].reshape(db // rb, rb, ng * ns),
                                  state_s5_im[j].reshape(db // rb, rb, ng * ns), wts,
                                  row0=tp, nchunk=db // rb, nseq=rb, slen=dl, ts=dl, prompt=False)
            s5_re.append((hr_p.reshape(nb, ng, ns), hr_s.reshape(db, ng, ns)))
            s5_im.append((hi_p.reshape(nb, ng, ns), hi_s.reshape(db, ng, ns)))
            y_mix = jnp.concatenate([y_p.reshape(tp, s5_ch), y_s], axis=0)
            o_mix = jnp.concatenate([o_p.reshape(tp, qkv_w), o_s.reshape(tsmp, qkv_w)], axis=0)
            w_out = w_out_even[j].astype(BF16)
            x = _linear([y_mix, o_mix], [w_out[:s5_ch], w_out[s5_ch:]], resid=(x, g1), name="out_proj_even")
        else:
            proj = _linear([x], [w_in_odd[j].astype(BF16)], norm=nm, tn=dr, name="in_proj_odd").reshape(t, -1)
            sp = jax.nn.softplus(-lru_lambda[j]).reshape(1, dr)
            wts = (conv_w[j], conv_b[j].reshape(1, dr), lru_w_a[j].astype(BF16), lru_b_a[j].reshape(1, dr),
                   lru_w_x[j].astype(BF16), lru_b_x[j].reshape(1, dr), sp)
            pad = SUBLANES - (kw - 1)
            y_p, buf_p, h_p = _lru(proj, jnp.zeros((1, nb, SUBLANES, dr), F32), jnp.zeros((1, nb, dr), F32), wts,
                                   row0=0, nchunk=1, nseq=nb, slen=sl, ts=ts_p, prompt=True)
            buf_s0 = jnp.pad(state_conv[j], ((0, 0), (pad, 0), (0, 0))).reshape(db // rb, rb, SUBLANES, dr)
            y_s, buf_s, h_s = _lru(proj, buf_s0, state_lru[j].reshape(db // rb, rb, dr), wts,
                                   row0=tp, nchunk=db // rb, nseq=rb, slen=dl, ts=dl, prompt=False)
            lru_h.append((h_p.reshape(nb, dr), h_s.reshape(db, dr)))
            conv_bufs.append((buf_p.reshape(nb, SUBLANES, dr)[:, pad:], buf_s.reshape(db, SUBLANES, dr)[:, pad:]))
            y_mix = jnp.concatenate([y_p.reshape(tp, dr), y_s], axis=0)
            x = _linear([y_mix], [w_out_odd[j].astype(BF16)], resid=(x, g1), name="out_proj_odd")

        keys = peer_keys[layer]
        kexp = [jnp.einsum("hkd,hg->khgd", keys[:, s], eye_h).reshape(nkeys * p_heads, -1).astype(BF16)
                for s in range(2)]
        hf, s1, s2 = _peer_scores(x, (norm_ffn[layer].reshape(1, 1, dm), sc2, sh2), peer_wq[layer].astype(BF16),
                                  kexp[0], kexp[1], nh=p_heads)
        e1, e2, gates = _peer_topk(s1, s2, nkeys=nkeys, nh=p_heads)
        x = _peer_dense(hf, e1, e2, gates, peer_u[layer].astype(BF16), peer_v[layer].astype(BF16), x, g2,
                        nkeys=nkeys)

    x2 = x.reshape(t, dm)
    y_prompt = x2[:tp].reshape(nb, sl, dm)
    y_sample = x2[tp:].reshape(db, dl, dm)
    kv_all = jnp.stack(kv_rows)
    kv_prompt = kv_all[:, :tp].reshape(n_even, nb, sl, 2, nh, 2 * hd)
    kv_sample = kv_all[:, tp:].reshape(n_even, db, dl, 2, nh, 2 * hd)
    pick = lambda lst, i: jnp.stack([e[i] for e in lst])
    return (y_prompt, y_sample, kv_prompt, kv_sample, pick(s5_re, 0), pick(s5_im, 0), pick(s5_re, 1),
            pick(s5_im, 1), pick(lru_h, 0), pick(lru_h, 1), pick(conv_bufs, 0), pick(conv_bufs, 1))
```
